```python
import math
import jax, jax.numpy as jnp
from jax import lax
import numpy as np

D_MODEL = 1024
BATCH = 16
SEQ = 4096
DEPTH = 1

CHUNK = 64
D_MIX = D_MODEL
D_CONV = D_MIX // 2
CONV_K = 3
D_SSM = D_MIX - D_CONV
SSM_HEADDIM = 64
SSM_HEADS = D_SSM // SSM_HEADDIM
SSM_GROUPS = 2
SSM_STATE = 128
SSM_CONV_K = 4
D_XBC = D_SSM + 2 * SSM_GROUPS * SSM_STATE
D_IN_PROJ = 3 * D_CONV + D_SSM + D_XBC + SSM_HEADS
N_EXPERTS = 64
TOP_K = 8
N_EXPERT_GROUPS = 8
TOPK_GROUPS = 4
D_EXPERT = 256
D_SHARED = 256
ROUTED_SCALE = 2.5
EPS = 1e-6

kernel_name = "hybrid_conv_ssd_moe_adaln_block"


def rms_norm(x, g):
    xf = x.astype(jnp.float32)
    y = xf * lax.rsqrt(jnp.mean(xf * xf, axis=-1, keepdims=True) + EPS)
    return (y * g.astype(jnp.float32)).astype(x.dtype)


def modulate(h, shift, scale):
    return h * (1 + scale[:, None, :]) + shift[:, None, :]


def causal_dwconv(u, w):
    k, ch = w.shape
    return lax.conv_general_dilated(
        u, w[:, None, :].astype(u.dtype), window_strides=(1,), padding=[(k - 1, 0)],
        dimension_numbers=("NWC", "WIO", "NWC"), feature_group_count=ch)


def segsum(a):
    cs = jnp.cumsum(a, axis=-1)
    n = a.shape[-1]
    diff = cs[..., :, None] - cs[..., None, :]
    return jnp.where(jnp.tril(jnp.ones((n, n), dtype=bool)), diff, -jnp.inf)


def ssd_chunked(xh, dt, A, Bm, Cm):
    b, s, h, p = xh.shape
    g, n = Bm.shape[-2:]
    r = h // g
    nc = s // CHUNK
    x = xh.reshape(b, nc, CHUNK, g, r, p)
    dtc = dt.reshape(b, nc, CHUNK, g, r)
    Bc = Bm.reshape(b, nc, CHUNK, g, n)
    Cc = Cm.reshape(b, nc, CHUNK, g, n)
    a = jnp.moveaxis(dtc * A.reshape(g, r), 2, -1)
    a_cs = jnp.cumsum(a, axis=-1)
    dtx = x * dtc[..., None]
    cb = jnp.einsum("bclgn,bcsgn->bcgls", Cc, Bc)
    wts = cb[:, :, :, None] * jnp.exp(segsum(a))
    y_diag = jnp.einsum("bcgrls,bcsgrp->bclgrp", wts, dtx)
    decay_l = jnp.moveaxis(jnp.exp(a_cs[..., -1:] - a_cs), -1, 2)
    states = jnp.einsum("bclgn,bclgrp->bcgrpn", Bc, dtx * decay_l[..., None])
    chunk_decay = jnp.exp(a_cs[..., -1])

    def step(prev, inp):
        st, dec = inp
        return prev * dec[..., None, None] + st, prev

    init = jnp.zeros((b, g, r, p, n), jnp.float32)
    _, prev_states = lax.scan(step, init, (jnp.moveaxis(states, 1, 0), jnp.moveaxis(chunk_decay, 1, 0)))
    prev_states = jnp.moveaxis(prev_states, 0, 1)
    out_decay = jnp.moveaxis(jnp.exp(a_cs), -1, 2)
    y_off = jnp.einsum("bclgn,bcgrpn->bclgrp", Cc, prev_states) * out_decay[..., None]
    return (y_diag + y_off).reshape(b, s, h, p)


def short_conv_mixer(gate_b, gate_c, u, w):
    return gate_b * causal_dwconv(gate_c * u, w)


def ssd_mixer(z, xbc, dt_raw, conv_w, conv_b, dt_bias, a_log, d_skip, norm_g):
    b, s, _ = z.shape
    xbc = jax.nn.silu(causal_dwconv(xbc, conv_w) + conv_b.astype(xbc.dtype))
    gn = SSM_GROUPS * SSM_STATE
    xs, Bm, Cm = jnp.split(xbc.astype(jnp.float32), [D_SSM, D_SSM + gn], axis=-1)
    dt = jax.nn.softplus(dt_raw.astype(jnp.float32) + dt_bias.astype(jnp.float32))
    A = -jnp.exp(a_log.astype(jnp.float32))
    xh = xs.reshape(b, s, SSM_HEADS, SSM_HEADDIM)
    y = ssd_chunked(xh, dt, A, Bm.reshape(b, s, SSM_GROUPS, SSM_STATE), Cm.reshape(b, s, SSM_GROUPS, SSM_STATE))
    y = y + d_skip.astype(jnp.float32)[:, None] * xh
    y = y.reshape(b, s, D_SSM) * jax.nn.silu(z.astype(jnp.float32))
    return rms_norm(y, norm_g).astype(z.dtype)


def moe_ffn(h, w_router, router_bias, we_gate, we_up, we_down, ws_gate, ws_up, ws_down):
    b, s, d = h.shape
    t = h.reshape(b * s, d)
    scores = jax.nn.sigmoid(jnp.dot(t.astype(jnp.float32), w_router.astype(jnp.float32)))
    sel = scores + router_bias.astype(jnp.float32)
    grp = sel.reshape(-1, N_EXPERT_GROUPS, N_EXPERTS // N_EXPERT_GROUPS)
    grp_score = jnp.sum(lax.top_k(grp, 2)[0], axis=-1)
    _, grp_idx = lax.top_k(grp_score, TOPK_GROUPS)
    grp_mask = jnp.any(grp_idx[..., None] == jnp.arange(N_EXPERT_GROUPS), axis=-2)
    exp_mask = jnp.repeat(grp_mask, N_EXPERTS // N_EXPERT_GROUPS, axis=-1)
    _, idx = lax.top_k(jnp.where(exp_mask, sel, -jnp.inf), TOP_K)
    w = jnp.take_along_axis(scores, idx, axis=-1)
    w = w / jnp.sum(w, axis=-1, keepdims=True) * ROUTED_SCALE
    gates = jnp.sum(jax.nn.one_hot(idx, N_EXPERTS, dtype=jnp.float32) * w[..., None], axis=1).astype(t.dtype)
    out = (jax.nn.silu(t @ ws_gate) * (t @ ws_up)) @ ws_down
    for e in range(N_EXPERTS):
        hid = jax.nn.silu(t @ we_gate[e]) * (t @ we_up[e])
        out = out + gates[:, e:e + 1] * (hid @ we_down[e])
    return out.reshape(b, s, d)


def setup_inputs(seed: int = 0) -> dict:
    key = jax.random.key(seed)
    ks = iter(jax.random.split(key, 32))
    L = DEPTH

    def nrm(shape, scale):
        return jax.random.normal(next(ks), shape, jnp.float32) * scale

    x = nrm((BATCH, SEQ, D_MODEL), 1.0)
    c = nrm((BATCH, D_MODEL), 1.0)
    w_ada = nrm((L, D_MODEL, 6 * D_MODEL), 0.5 * D_MODEL ** -0.5)
    b_ada = nrm((L, 6 * D_MODEL), 0.02)
    norm_mix_g = 1.0 + nrm((L, D_MODEL), 0.02)
    w_in = nrm((L, D_MODEL, D_IN_PROJ), D_MODEL ** -0.5)
    conv_a_w = nrm((L, CONV_K, D_CONV), CONV_K ** -0.5)
    ssm_conv_w = nrm((L, SSM_CONV_K, D_XBC), SSM_CONV_K ** -0.5)
    ssm_conv_b = nrm((L, D_XBC), 0.02)
    dt0 = jnp.exp(jax.random.uniform(next(ks), (L, SSM_HEADS), jnp.float32, math.log(1e-3), math.log(1e-1)))
    ssm_dt_bias = dt0 + jnp.log(-jnp.expm1(-dt0))
    ssm_a_log = jnp.log(jax.random.uniform(next(ks), (L, SSM_HEADS), jnp.float32, 1.0, 16.0))
    ssm_d = 1.0 + nrm((L, SSM_HEADS), 0.1)
    ssm_norm_g = 1.0 + nrm((L, D_SSM), 0.02)
    w_out = nrm((L, D_MIX, D_MODEL), D_MIX ** -0.5)
    norm_ffn_g = 1.0 + nrm((L, D_MODEL), 0.02)
    w_router = nrm((L, D_MODEL, N_EXPERTS), D_MODEL ** -0.5)
    router_bias = nrm((L, N_EXPERTS), 0.01)
    we_gate = nrm((L, N_EXPERTS, D_MODEL, D_EXPERT), D_MODEL ** -0.5)
    we_up = nrm((L, N_EXPERTS, D_MODEL, D_EXPERT), D_MODEL ** -0.5)
    we_down = nrm((L, N_EXPERTS, D_EXPERT, D_MODEL), D_EXPERT ** -0.5)
    ws_gate = nrm((L, D_MODEL, D_SHARED), D_MODEL ** -0.5)
    ws_up = nrm((L, D_MODEL, D_SHARED), D_MODEL ** -0.5)
    ws_down = nrm((L, D_SHARED, D_MODEL), D_SHARED ** -0.5)
    norm_final_g = 1.0 + nrm((D_MODEL,), 0.02)
    return {"x": x, "c": c, "w_ada": w_ada, "b_ada": b_ada, "norm_mix_g": norm_mix_g,
            "w_in": w_in, "conv_a_w": conv_a_w, "ssm_conv_w": ssm_conv_w, "ssm_conv_b": ssm_conv_b,
            "ssm_dt_bias": ssm_dt_bias, "ssm_a_log": ssm_a_log, "ssm_d": ssm_d, "ssm_norm_g": ssm_norm_g,
            "w_out": w_out, "norm_ffn_g": norm_ffn_g, "w_router": w_router, "router_bias": router_bias,
            "we_gate": we_gate, "we_up": we_up, "we_down": we_down,
            "ws_gate": ws_gate, "ws_up": ws_up, "ws_down": ws_down, "norm_final_g": norm_final_g}


def reference(x, c, w_ada, b_ada, norm_mix_g, w_in, conv_a_w, ssm_conv_w, ssm_conv_b,
              ssm_dt_bias, ssm_a_log, ssm_d, ssm_norm_g, w_out, norm_ffn_g, w_router, router_bias,
              we_gate, we_up, we_down, ws_gate, ws_up, ws_down, norm_final_g):
    gn = SSM_GROUPS * SSM_STATE
    cuts = [D_CONV, 2 * D_CONV, 3 * D_CONV, 3 * D_CONV + D_SSM, 3 * D_CONV + D_SSM + D_XBC]
    for l in range(DEPTH):
        mod = (jnp.dot(c.astype(jnp.float32), w_ada[l].astype(jnp.float32)) + b_ada[l].astype(jnp.float32)).astype(x.dtype)
        sh1, sc1, g1, sh2, sc2, g2 = jnp.split(mod, 6, axis=-1)
        h = modulate(rms_norm(x, norm_mix_g[l]), sh1, sc1)
        proj = h @ w_in[l]
        cb, cc, cu, z, xbc, dt_raw = jnp.split(proj, cuts, axis=-1)
        y_a = short_conv_mixer(cb, cc, cu, conv_a_w[l])
        y_b = ssd_mixer(z, xbc, dt_raw, ssm_conv_w[l], ssm_conv_b[l], ssm_dt_bias[l],
                        ssm_a_log[l], ssm_d[l], ssm_norm_g[l])
        x = x + g1[:, None, :] * (jnp.concatenate([y_a, y_b], axis=-1) @ w_out[l])
        h = modulate(rms_norm(x, norm_ffn_g[l]), sh2, sc2)
        x = x + g2[:, None, :] * moe_ffn(h, w_router[l], router_bias[l], we_gate[l], we_up[l], we_down[l],
                                         ws_gate[l], ws_up[l], ws_down[l])
    return rms_norm(x, norm_final_g)
```

```python
import functools

import jax
import jax.numpy as jnp
from jax import lax
from jax.experimental import pallas as pl
from jax.experimental.pallas import tpu as pltpu

D_MODEL = 1024
D_CONV = 512
D_SSM = 512
SSM_HEADS = 8
SSM_HEADDIM = 64
SSM_GROUPS = 2
SSM_STATE = 128
HEADS_PER_GROUP = SSM_HEADS // SSM_GROUPS
GROUP_W = HEADS_PER_GROUP * SSM_HEADDIM
D_XBC = D_SSM + 2 * SSM_GROUPS * SSM_STATE
CHUNK = 64
N_EXPERTS = 64
N_EXPERT_GROUPS = 8
EXPERTS_PER_GROUP = N_EXPERTS // N_EXPERT_GROUPS
TOPK_GROUPS = 4
TOP_K = 8
D_EXPERT = 256
ROUTED_SCALE = 2.5
EPS = 1e-6
LANES = 128
SUBLANES = 8
VMEM_LIMIT = 56 * 1024 * 1024

F32 = jnp.float32
BF16 = jnp.bfloat16
HI = lax.Precision.HIGHEST


def _dot(a, b, precision=None):
    return jnp.dot(a, b, preferred_element_type=F32, precision=precision)


def _dot_nt(a, b, precision=None):
    return lax.dot_general(a, b, (((1,), (1,)), ((), ())), preferred_element_type=F32,
                           precision=precision)


def _dot_tn(a, b):
    return lax.dot_general(a, b, (((0,), (0,)), ((), ())), preferred_element_type=F32)


def _sigmoid(v):
    return 1.0 / (1.0 + jnp.exp(-v))


def _silu(v):
    return v * _sigmoid(v)


def _rms(v):
    return v * lax.rsqrt(jnp.mean(v * v, axis=-1, keepdims=True) + EPS)


def _ada_kernel(c_ref, w_ref, b_ref, o_ref):
    o_ref[0] = _dot(c_ref[...], w_ref[...], HI) + b_ref[...]


def _ada_call(c, w_ada, b_ada):
    bsz = c.shape[0]
    return pl.pallas_call(
        _ada_kernel,
        grid=(6,),
        in_specs=[pl.BlockSpec((bsz, D_MODEL), lambda j: (0, 0)),
                  pl.BlockSpec((D_MODEL, D_MODEL), lambda j: (0, j)),
                  pl.BlockSpec((1, D_MODEL), lambda j: (0, j))],
        out_specs=pl.BlockSpec((1, bsz, D_MODEL), lambda j: (j, 0, 0)),
        out_shape=jax.ShapeDtypeStruct((6, bsz, D_MODEL), F32),
        compiler_params=pltpu.CompilerParams(dimension_semantics=("arbitrary",),
                                             vmem_limit_bytes=VMEM_LIMIT),
    )(c, w_ada, b_ada.reshape(1, -1))


def _shift_rows(v, halo, k):
    rolled = pltpu.roll(v, k, 0)
    hrolled = pltpu.roll(halo, k, 0)
    rid = lax.broadcasted_iota(jnp.int32, (SUBLANES, v.shape[1]), 0)
    top = jnp.where(rid < k, hrolled, rolled[:SUBLANES])
    return jnp.concatenate([top, rolled[SUBLANES:]], axis=0)


def _mix_kernel(x_ref, mod_ref, gmix_ref, gffn_ref, win_ref, wdt_ref, cwa_ref, cws_ref, cbs_ref,
                dtb_ref, aneg_ref, dskip_ref, gssm_ref, wouta_ref, woutb_ref, wrt_ref,
                x1_ref, h2_ref, lg_ref,
                halo_v, halo_x, st_ref, xbc_s, dt_s, y_s, *, ts):
    s_idx = pl.program_id(1)

    @pl.when(s_idx == 0)
    def _():
        halo_v[...] = jnp.zeros_like(halo_v)
        halo_x[...] = jnp.zeros_like(halo_x)
        st_ref[...] = jnp.zeros_like(st_ref)

    x = x_ref[0]
    mod = mod_ref[0]
    h = _rms(x) * (gmix_ref[...] * (1.0 + mod[1:2])) + mod[0:1]
    hb = h.astype(BF16)

    def proj(lo, hi):
        return _dot(hb, win_ref[:, lo:hi])

    v = proj(D_CONV, 2 * D_CONV) * proj(2 * D_CONV, 3 * D_CONV)
    hv = halo_v[...]
    cwa = cwa_ref[...]
    conv_a = (cwa[2:3] * v + cwa[1:2] * _shift_rows(v, hv, 1) + cwa[0:1] * _shift_rows(v, hv, 2))
    halo_v[...] = v[ts - SUBLANES:]
    ya = (proj(0, D_CONV) * conv_a).astype(BF16)

    u = proj(4 * D_CONV, 4 * D_CONV + D_XBC)
    hx = halo_x[...]
    cws = cws_ref[...]
    conv_s = (cws[3:4] * u + cws[2:3] * _shift_rows(u, hx, 1) + cws[1:2] * _shift_rows(u, hx, 2)
              + cws[0:1] * _shift_rows(u, hx, 3)) + cbs_ref[...]
    halo_x[...] = u[ts - SUBLANES:]
    xbc_s[...] = _silu(conv_s)
    dt_raw = _dot(hb, wdt_ref[...]) + dtb_ref[...]
    dt_s[...] = jnp.maximum(dt_raw, 0.0) + jnp.log1p(jnp.exp(-jnp.abs(dt_raw)))

    li = lax.broadcasted_iota(jnp.int32, (CHUNK, D_SSM), 0)
    sj = lax.broadcasted_iota(jnp.int32, (CHUNK, D_SSM), 1) & (SSM_HEADDIM - 1)
    causal = sj <= li
    upto = (li <= sj).astype(F32)
    eh = lax.broadcasted_iota(jnp.int32, (LANES, D_SSM), 0)
    ej = lax.broadcasted_iota(jnp.int32, (LANES, D_SSM), 1) // SSM_HEADDIM
    expand = (eh == ej).astype(F32)
    tl = lax.broadcasted_iota(jnp.int32, (CHUNK, CHUNK), 0)
    tc = lax.broadcasted_iota(jnp.int32, (CHUNK, CHUNK), 1)
    tril = (tc <= tl).astype(F32)
    bi = lax.broadcasted_iota(jnp.int32, (GROUP_W, GROUP_W), 0) // SSM_HEADDIM
    bj = lax.broadcasted_iota(jnp.int32, (GROUP_W, GROUP_W), 1) // SSM_HEADDIM
    blockmask = (bi == bj).astype(F32)
    aneg = aneg_ref[...]
    dskip = dskip_ref[...]

    def chunk_body(c, carry):
        r0 = pl.multiple_of(c * CHUNK, CHUNK)
        xs = xbc_s[pl.ds(r0, CHUNK), 0:D_SSM]
        bm = xbc_s[pl.ds(r0, CHUNK), D_SSM:D_SSM + SSM_GROUPS * SSM_STATE]
        cm = xbc_s[pl.ds(r0, CHUNK), D_SSM + SSM_GROUPS * SSM_STATE:D_XBC]
        dt = dt_s[pl.ds(r0, CHUNK), :]
        a_exp = _dot(dt * aneg, expand, HI)
        dt_exp = _dot(dt, expand, HI)
        cs = _dot(tril, a_exp, HI)
        cs_row = jnp.sum(a_exp * upto, axis=0, keepdims=True)
        lmat = jnp.where(causal, jnp.exp(cs - cs_row), 0.0)
        dtx = xs * dt_exp
        out_decay = jnp.exp(cs)
        cs_last = cs[CHUNK - 1:CHUNK]
        end_decay = jnp.exp(cs_last - cs)
        chunk_decay = jnp.exp(cs_last)
        ys = []
        for g in range(SSM_GROUPS):
            lo, hi = g * GROUP_W, (g + 1) * GROUP_W
            bg = bm[:, g * SSM_STATE:(g + 1) * SSM_STATE].astype(BF16)
            cg = cm[:, g * SSM_STATE:(g + 1) * SSM_STATE].astype(BF16)
            brep = jnp.concatenate([bg] * HEADS_PER_GROUP, axis=0)
            w = (_dot_nt(cg, brep) * lmat[:, lo:hi]).astype(BF16)
            dtx_g = dtx[:, lo:hi]
            bd = (jnp.concatenate([dtx_g] * HEADS_PER_GROUP, axis=0) * blockmask).astype(BF16)
            st = st_ref[g]
            y_g = _dot(w, bd) + _dot(cg, st.astype(BF16)) * out_decay[:, lo:hi]
            new = _dot_tn(bg, (dtx_g * end_decay[:, lo:hi]).astype(BF16))
            st_ref[g] = st * chunk_decay[:, lo:hi] + new
            ys.append(y_g)
        y_s[pl.ds(r0, CHUNK), :] = jnp.concatenate(ys, axis=1) + dskip * xs
        return carry

    lax.fori_loop(0, ts // CHUNK, chunk_body, 0)

    z = proj(3 * D_CONV, 4 * D_CONV)
    yb = (_rms(y_s[...] * _silu(z)) * gssm_ref[...]).astype(BF16)

    out = _dot(ya, wouta_ref[...]) + _dot(yb, woutb_ref[...])
    x1 = x + mod[2:3] * out
    x1_ref[0] = x1
    h2 = _rms(x1) * (gffn_ref[...] * (1.0 + mod[4:5])) + mod[3:4]
    h2_ref[0] = h2.astype(BF16)
    lg_ref[...] = _dot_nt(wrt_ref[...], h2, HI)


def _mix_call(x, modrows, gmix, gffn, w_main, w_dt, cwa, cws, cbs, dtb, aneg, dskip, gssm,
              wout_a, wout_b, wr_t, ts):
    bsz, seq, _ = x.shape
    ns = seq // ts
    const = lambda shape: pl.BlockSpec(shape, lambda b, s: (0,) * len(shape))
    return pl.pallas_call(
        functools.partial(_mix_kernel, ts=ts),
        grid=(bsz, ns),
        in_specs=[pl.BlockSpec((1, ts, D_MODEL), lambda b, s: (b, s, 0)),
                  pl.BlockSpec((1, SUBLANES, D_MODEL), lambda b, s: (b, 0, 0)),
                  const(gmix.shape), const(gffn.shape), const(w_main.shape), const(w_dt.shape),
                  const(cwa.shape), const(cws.shape), const(cbs.shape), const(dtb.shape),
                  const(aneg.shape), const(dskip.shape), const(gssm.shape),
                  const(wout_a.shape), const(wout_b.shape), const(wr_t.shape)],
        out_specs=[pl.BlockSpec((1, ts, D_MODEL), lambda b, s: (b, s, 0)),
                   pl.BlockSpec((1, ts, D_MODEL), lambda b, s: (b, s, 0)),
                   pl.BlockSpec((N_EXPERTS, ts), lambda b, s: (0, b * ns + s))],
        out_shape=[jax.ShapeDtypeStruct((bsz, seq, D_MODEL), F32),
                   jax.ShapeDtypeStruct((bsz, seq, D_MODEL), BF16),
                   jax.ShapeDtypeStruct((N_EXPERTS, bsz * seq), F32)],
        scratch_shapes=[pltpu.VMEM((SUBLANES, D_CONV), F32),
                        pltpu.VMEM((SUBLANES, D_XBC), F32),
                        pltpu.VMEM((SSM_GROUPS, SSM_STATE, GROUP_W), F32),
                        pltpu.VMEM((ts, D_XBC), F32),
                        pltpu.VMEM((ts, LANES), F32),
                        pltpu.VMEM((ts, D_SSM), F32)],
        compiler_params=pltpu.CompilerParams(dimension_semantics=("arbitrary", "arbitrary"),
                                             vmem_limit_bytes=VMEM_LIMIT),
    )(x, modrows, gmix, gffn, w_main, w_dt, cwa, cws, cbs, dtb, aneg, dskip, gssm,
      wout_a, wout_b, wr_t)


def _route_kernel(lg_ref, bias_ref, gate_ref):
    scores = _sigmoid(lg_ref[...])
    sel = scores + bias_ref[...]
    n = sel.shape[1]
    neg = -jnp.inf
    idx8 = lax.broadcasted_iota(jnp.int32, (EXPERTS_PER_GROUP, n), 0)
    grp_scores = []
    for g in range(N_EXPERT_GROUPS):
        sg = sel[g * EXPERTS_PER_GROUP:(g + 1) * EXPERTS_PER_GROUP]
        m1 = jnp.max(sg, axis=0, keepdims=True)
        first = jnp.min(jnp.where(sg == m1, idx8, EXPERTS_PER_GROUP), axis=0, keepdims=True)
        m2 = jnp.max(jnp.where(idx8 == first, neg, sg), axis=0, keepdims=True)
        grp_scores.append(m1 + m2)
    masked = []
    for g in range(N_EXPERT_GROUPS):
        rank = jnp.zeros((1, n), jnp.int32)
        for o in range(N_EXPERT_GROUPS):
            if o == g:
                continue
            ahead = (grp_scores[o] >= grp_scores[g]) if o < g else (grp_scores[o] > grp_scores[g])
            rank = rank + ahead.astype(jnp.int32)
        sg = sel[g * EXPERTS_PER_GROUP:(g + 1) * EXPERTS_PER_GROUP]
        masked.append(jnp.where(rank < TOPK_GROUPS, sg, neg))
    vals = jnp.concatenate(masked, axis=0)
    idx = lax.broadcasted_iota(jnp.int32, (N_EXPERTS, n), 0)
    picked = jnp.zeros((N_EXPERTS, n), F32)
    for _ in range(TOP_K):
        m = jnp.max(vals, axis=0, keepdims=True)
        first = jnp.min(jnp.where(vals == m, idx, N_EXPERTS), axis=0, keepdims=True)
        hit = idx == first
        picked = jnp.where(hit, scores, picked)
        vals = jnp.where(hit, neg, vals)
    gate_ref[...] = picked / jnp.sum(picked, axis=0, keepdims=True) * ROUTED_SCALE


def _route_call(logits_t, bias_col, nl):
    n_tok = logits_t.shape[1]
    return pl.pallas_call(
        _route_kernel,
        grid=(n_tok // nl,),
        in_specs=[pl.BlockSpec((N_EXPERTS, nl), lambda i: (0, i)),
                  pl.BlockSpec((N_EXPERTS, 1), lambda i: (0, 0))],
        out_specs=pl.BlockSpec((N_EXPERTS, nl), lambda i: (0, i)),
        out_shape=jax.ShapeDtypeStruct((N_EXPERTS, n_tok), F32),
        compiler_params=pltpu.CompilerParams(dimension_semantics=("arbitrary",),
                                             vmem_limit_bytes=VMEM_LIMIT),
    )(logits_t, bias_col)


def _moe_kernel(h2_ref, gates_ref, x1_ref, mod_ref, wg_ref, wu_ref, wd_ref, sg_ref, su_ref,
                sd_ref, gfin_ref, o_ref, acc_ref):
    e = pl.program_id(1)
    t = h2_ref[...]

    @pl.when(e == 0)
    def _():
        hid = _silu(_dot(t, sg_ref[...])) * _dot(t, su_ref[...])
        acc_ref[...] = _dot(hid.astype(BF16), sd_ref[...])

    lane = lax.broadcasted_iota(jnp.int32, gates_ref.shape, 1)
    gcol = jnp.sum(jnp.where(lane == e, gates_ref[...], 0.0), axis=1, keepdims=True)
    hid = _silu(_dot(t, wg_ref[0])) * _dot(t, wu_ref[0])
    acc_ref[...] += _dot((hid * gcol).astype(BF16), wd_ref[0])

    @pl.when(e == pl.num_programs(1) - 1)
    def _():
        x2 = x1_ref[...] + mod_ref[0][5:6] * acc_ref[...]
        o_ref[...] = _rms(x2) * gfin_ref[...]


def _moe_call(h2, gates, x1, modrows, wg, wu, wd, sg, su, sd, gfin, tm, seq):
    n_tok = h2.shape[0]
    tiles_per_batch = seq // tm
    const = lambda shape: pl.BlockSpec(shape, lambda i, e: (0,) * len(shape))
    return pl.pallas_call(
        _moe_kernel,
        grid=(n_tok // tm, N_EXPERTS),
        in_specs=[pl.BlockSpec((tm, D_MODEL), lambda i, e: (i, 0)),
                  pl.BlockSpec((tm, N_EXPERTS), lambda i, e: (i, 0)),
                  pl.BlockSpec((tm, D_MODEL), lambda i, e: (i, 0)),
                  pl.BlockSpec((1, SUBLANES, D_MODEL), lambda i, e: (i // tiles_per_batch, 0, 0)),
                  pl.BlockSpec((1, D_MODEL, D_EXPERT), lambda i, e: (e, 0, 0)),
                  pl.BlockSpec((1, D_MODEL, D_EXPERT), lambda i, e: (e, 0, 0)),
                  pl.BlockSpec((1, D_EXPERT, D_MODEL), lambda i, e: (e, 0, 0)),
                  const(sg.shape), const(su.shape), const(sd.shape), const(gfin.shape)],
        out_specs=pl.BlockSpec((tm, D_MODEL), lambda i, e: (i, 0)),
        out_shape=jax.ShapeDtypeStruct((n_tok, D_MODEL), F32),
        scratch_shapes=[pltpu.VMEM((tm, D_MODEL), F32)],
        compiler_params=pltpu.CompilerParams(dimension_semantics=("arbitrary", "arbitrary"),
                                             vmem_limit_bytes=VMEM_LIMIT),
    )(h2, gates, x1, modrows, wg, wu, wd, sg, su, sd, gfin)


def _tile(n, pref):
    return pref if n % pref == 0 else n


def kernel(x, c, w_ada, b_ada, norm_mix_g, w_in, conv_a_w, ssm_conv_w, ssm_conv_b, ssm_dt_bias,
           ssm_a_log, ssm_d, ssm_norm_g, w_out, norm_ffn_g, w_router, router_bias, we_gate, we_up,
           we_down, ws_gate, ws_up, ws_down, norm_final_g):
    bsz, seq, _ = x.shape
    n_tok = bsz * seq
    assert w_ada.shape[0] == 1, "single-layer block"
    l = 0
    n_main = 4 * D_CONV + D_XBC
    pad_heads = lambda a: jnp.pad(a.reshape(1, -1), ((0, 0), (0, LANES - SSM_HEADS)))
    mod = _ada_call(c, w_ada[l], b_ada[l])
    modrows = jnp.pad(jnp.transpose(mod, (1, 0, 2)), ((0, 0), (0, SUBLANES - 6), (0, 0)))
    w_main = w_in[l][:, :n_main].astype(BF16)
    w_dt = jnp.pad(w_in[l][:, n_main:], ((0, 0), (0, LANES - SSM_HEADS))).astype(BF16)
    x1, h2, logits_t = _mix_call(
        x, modrows, norm_mix_g[l].reshape(1, -1), norm_ffn_g[l].reshape(1, -1), w_main, w_dt,
        conv_a_w[l], ssm_conv_w[l], ssm_conv_b[l].reshape(1, -1), pad_heads(ssm_dt_bias[l]),
        pad_heads(-jnp.exp(ssm_a_log[l])), jnp.repeat(ssm_d[l], SSM_HEADDIM).reshape(1, -1),
        ssm_norm_g[l].reshape(1, -1), w_out[l][:D_CONV].astype(BF16),
        w_out[l][D_CONV:].astype(BF16), jnp.transpose(w_router[l]), _tile(seq, 256))
    gates_t = _route_call(logits_t, router_bias[l].reshape(-1, 1), _tile(n_tok, 512))
    out = _moe_call(h2.reshape(n_tok, D_MODEL), jnp.transpose(gates_t),
                    x1.reshape(n_tok, D_MODEL), modrows, we_gate[l].astype(BF16),
                    we_up[l].astype(BF16), we_down[l].astype(BF16), ws_gate[l].astype(BF16),
                    ws_up[l].astype(BF16), ws_down[l].astype(BF16),
                    norm_final_g.reshape(1, -1), _tile(seq, 512), seq)
    return out.reshape(bsz, seq, D_MODEL)
```

```python
import functools

import jax
import jax.numpy as jnp
from jax import lax
from jax.experimental import pallas as pl
from jax.experimental.pallas import tpu as pltpu

D_MODEL = 1024
D_CONV = 512
D_SSM = 512
SSM_HEADS = 8
SSM_HEADDIM = 64
SSM_GROUPS = 2
SSM_STATE = 128
HEADS_PER_GROUP = SSM_HEADS // SSM_GROUPS
GROUP_W = HEADS_PER_GROUP * SSM_HEADDIM
D_XBC = D_SSM + 2 * SSM_GROUPS * SSM_STATE
CHUNK = 64
N_EXPERTS = 64
N_EXPERT_GROUPS = 8
EXPERTS_PER_GROUP = N_EXPERTS // N_EXPERT_GROUPS
TOPK_GROUPS = 4
TOP_K = 8
D_EXPERT = 256
ROUTED_SCALE = 2.5
EPS = 1e-6
LANES = 128
SUBLANES = 8
VMEM_LIMIT = 56 * 1024 * 1024

F32 = jnp.float32
BF16 = jnp.bfloat16
HI = lax.Precision.HIGHEST


def _dot(a, b, precision=None):
    return jnp.dot(a, b, preferred_element_type=F32, precision=precision)


def _dot_nt(a, b, precision=None):
    return lax.dot_general(a, b, (((1,), (1,)), ((), ())), preferred_element_type=F32,
                           precision=precision)


def _dot_tn(a, b):
    return lax.dot_general(a, b, (((0,), (0,)), ((), ())), preferred_element_type=F32)


def _sigmoid(v):
    return 1.0 / (1.0 + jnp.exp(-v))


def _silu(v):
    return v * _sigmoid(v)


def _rms(v):
    return v * lax.rsqrt(jnp.mean(v * v, axis=-1, keepdims=True) + EPS)


def _ada_kernel(c_ref, w_ref, b_ref, o_ref):
    o_ref[0] = _dot(c_ref[...], w_ref[...], HI) + b_ref[...]


def _ada_call(c, w_ada, b_ada):
    bsz = c.shape[0]
    return pl.pallas_call(
        _ada_kernel,
        grid=(6,),
        in_specs=[pl.BlockSpec((bsz, D_MODEL), lambda j: (0, 0)),
                  pl.BlockSpec((D_MODEL, D_MODEL), lambda j: (0, j)),
                  pl.BlockSpec((1, D_MODEL), lambda j: (0, j))],
        out_specs=pl.BlockSpec((1, bsz, D_MODEL), lambda j: (j, 0, 0)),
        out_shape=jax.ShapeDtypeStruct((6, bsz, D_MODEL), F32),
        compiler_params=pltpu.CompilerParams(dimension_semantics=("arbitrary",),
                                             vmem_limit_bytes=VMEM_LIMIT),
    )(c, w_ada, b_ada.reshape(1, -1))


def _shift_rows(v, halo, k):
    rolled = pltpu.roll(v, k, 0)
    hrolled = pltpu.roll(halo, k, 0)
    rid = lax.broadcasted_iota(jnp.int32, (SUBLANES, v.shape[1]), 0)
    top = jnp.where(rid < k, hrolled, rolled[:SUBLANES])
    return jnp.concatenate([top, rolled[SUBLANES:]], axis=0)


def _mix_kernel(x_ref, mod_ref, gmix_ref, gffn_ref, win_ref, wdt_ref, cwa_ref, cws_ref, cbs_ref,
                dtb_ref, aneg_ref, dskip_ref, gssm_ref, wouta_ref, woutb_ref, wrt_ref,
                x1_ref, h2_ref, lg_ref,
                halo_v, halo_x, st_ref, xbc_s, dt_s, y_s, *, ts):
    s_idx = pl.program_id(1)

    @pl.when(s_idx == 0)
    def _():
        halo_v[...] = jnp.zeros_like(halo_v)
        halo_x[...] = jnp.zeros_like(halo_x)
        st_ref[...] = jnp.zeros_like(st_ref)

    x = x_ref[0]
    mod = mod_ref[0]
    h = _rms(x) * (gmix_ref[...] * (1.0 + mod[1:2])) + mod[0:1]
    hb = h.astype(BF16)

    def proj(lo, hi):
        return _dot(hb, win_ref[:, lo:hi])

    v = proj(D_CONV, 2 * D_CONV) * proj(2 * D_CONV, 3 * D_CONV)
    hv = halo_v[...]
    cwa = cwa_ref[...]
    conv_a = (cwa[2:3] * v + cwa[1:2] * _shift_rows(v, hv, 1) + cwa[0:1] * _shift_rows(v, hv, 2))
    halo_v[...] = v[ts - SUBLANES:]
    ya = (proj(0, D_CONV) * conv_a).astype(BF16)

    u = proj(4 * D_CONV, 4 * D_CONV + D_XBC)
    hx = halo_x[...]
    cws = cws_ref[...]
    conv_s = (cws[3:4] * u + cws[2:3] * _shift_rows(u, hx, 1) + cws[1:2] * _shift_rows(u, hx, 2)
              + cws[0:1] * _shift_rows(u, hx, 3)) + cbs_ref[...]
    halo_x[...] = u[ts - SUBLANES:]
    xbc_s[...] = _silu(conv_s)
    dt_raw = _dot(hb, wdt_ref[...]) + dtb_ref[...]
    dt_s[...] = jnp.maximum(dt_raw, 0.0) + jnp.log1p(jnp.exp(-jnp.abs(dt_raw)))

    li = lax.broadcasted_iota(jnp.int32, (CHUNK, D_SSM), 0)
    sj = lax.broadcasted_iota(jnp.int32, (CHUNK, D_SSM), 1) & (SSM_HEADDIM - 1)
    causal = sj <= li
    upto = (li <= sj).astype(F32)
    eh = lax.broadcasted_iota(jnp.int32, (LANES, D_SSM), 0)
    ej = lax.broadcasted_iota(jnp.int32, (LANES, D_SSM), 1) // SSM_HEADDIM
    expand = (eh == ej).astype(F32)
    tl = lax.broadcasted_iota(jnp.int32, (CHUNK, CHUNK), 0)
    tc = lax.broadcasted_iota(jnp.int32, (CHUNK, CHUNK), 1)
    tril = (tc <= tl).astype(F32)
    bi = lax.broadcasted_iota(jnp.int32, (GROUP_W, GROUP_W), 0) // SSM_HEADDIM
    bj = lax.broadcasted_iota(jnp.int32, (GROUP_W, GROUP_W), 1) // SSM_HEADDIM
    blockmask = (bi == bj).astype(F32)
    aneg = aneg_ref[...]
    dskip = dskip_ref[...]

    def chunk_body(c, carry):
        r0 = pl.multiple_of(c * CHUNK, CHUNK)
        xs = xbc_s[pl.ds(r0, CHUNK), 0:D_SSM]
        bm = xbc_s[pl.ds(r0, CHUNK), D_SSM:D_SSM + SSM_GROUPS * SSM_STATE]
        cm = xbc_s[pl.ds(r0, CHUNK), D_SSM + SSM_GROUPS * SSM_STATE:D_XBC]
        dt = dt_s[pl.ds(r0, CHUNK), :]
        a_exp = _dot(dt * aneg, expand, HI)
        dt_exp = _dot(dt, expand, HI)
        cs = _dot(tril, a_exp, HI)
        cs_row = jnp.sum(a_exp * upto, axis=0, keepdims=True)
        lmat = jnp.where(causal, jnp.exp(cs - cs_row), 0.0)
        dtx = xs * dt_exp
        out_decay = jnp.exp(cs)
        cs_last = cs[CHUNK - 1:CHUNK]
        end_decay = jnp.exp(cs_last - cs)
        chunk_decay = jnp.exp(cs_last)
        ys = []
        for g in range(SSM_GROUPS):
            lo, hi = g * GROUP_W, (g + 1) * GROUP_W
            bg = bm[:, g * SSM_STATE:(g + 1) * SSM_STATE].astype(BF16)
            cg = cm[:, g * SSM_STATE:(g + 1) * SSM_STATE].astype(BF16)
            brep = jnp.concatenate([bg] * HEADS_PER_GROUP, axis=0)
            w = (_dot_nt(cg, brep) * lmat[:, lo:hi]).astype(BF16)
            dtx_g = dtx[:, lo:hi]
            bd = (jnp.concatenate([dtx_g] * HEADS_PER_GROUP, axis=0) * blockmask).astype(BF16)
            st = st_ref[g]
            y_g = _dot(w, bd) + _dot(cg, st.astype(BF16)) * out_decay[:, lo:hi]
            new = _dot_tn(bg, (dtx_g * end_decay[:, lo:hi]).astype(BF16))
            st_ref[g] = st * chunk_decay[:, lo:hi] + new
            ys.append(y_g)
        y_s[pl.ds(r0, CHUNK), :] = jnp.concatenate(ys, axis=1) + dskip * xs
        return carry

    lax.fori_loop(0, ts // CHUNK, chunk_body, 0)

    z = proj(3 * D_CONV, 4 * D_CONV)
    yb = (_rms(y_s[...] * _silu(z)) * gssm_ref[...]).astype(BF16)

    out = _dot(ya, wouta_ref[...]) + _dot(yb, woutb_ref[...])
    x1 = x + mod[2:3] * out
    x1_ref[0] = x1
    h2 = _rms(x1) * (gffn_ref[...] * (1.0 + mod[4:5])) + mod[3:4]
    h2_ref[0] = h2.astype(BF16)
    lg_ref[...] = _dot_nt(wrt_ref[...], h2, HI)


def _mix_call(x, modrows, gmix, gffn, w_main, w_dt, cwa, cws, cbs, dtb, aneg, dskip, gssm,
              wout_a, wout_b, wr_t, ts):
    bsz, seq, _ = x.shape
    ns = seq // ts
    const = lambda shape: pl.BlockSpec(shape, lambda b, s: (0,) * len(shape))
    return pl.pallas_call(
        functools.partial(_mix_kernel, ts=ts),
        grid=(bsz, ns),
        in_specs=[pl.BlockSpec((1, ts, D_MODEL), lambda b, s: (b, s, 0)),
                  pl.BlockSpec((1, SUBLANES, D_MODEL), lambda b, s: (b, 0, 0)),
                  const(gmix.shape), const(gffn.shape), const(w_main.shape), const(w_dt.shape),
                  const(cwa.shape), const(cws.shape), const(cbs.shape), const(dtb.shape),
                  const(aneg.shape), const(dskip.shape), const(gssm.shape),
                  const(wout_a.shape), const(wout_b.shape), const(wr_t.shape)],
        out_specs=[pl.BlockSpec((1, ts, D_MODEL), lambda b, s: (b, s, 0)),
                   pl.BlockSpec((1, ts, D_MODEL), lambda b, s: (b, s, 0)),
                   pl.BlockSpec((N_EXPERTS, ts), lambda b, s: (0, b * ns + s))],
        out_shape=[jax.ShapeDtypeStruct((bsz, seq, D_MODEL), F32),
                   jax.ShapeDtypeStruct((bsz, seq, D_MODEL), BF16),
                   jax.ShapeDtypeStruct((N_EXPERTS, bsz * seq), F32)],
        scratch_shapes=[pltpu.VMEM((SUBLANES, D_CONV), F32),
                        pltpu.VMEM((SUBLANES, D_XBC), F32),
                        pltpu.VMEM((SSM_GROUPS, SSM_STATE, GROUP_W), F32),
                        pltpu.VMEM((ts, D_XBC), F32),
                        pltpu.VMEM((ts, LANES), F32),
                        pltpu.VMEM((ts, D_SSM), F32)],
        compiler_params=pltpu.CompilerParams(dimension_semantics=("arbitrary", "arbitrary"),
                                             vmem_limit_bytes=VMEM_LIMIT),
    )(x, modrows, gmix, gffn, w_main, w_dt, cwa, cws, cbs, dtb, aneg, dskip, gssm,
      wout_a, wout_b, wr_t)


def _route_kernel(lg_ref, bias_ref, gate_ref, cnt_ref):
    scores = _sigmoid(lg_ref[...])
    sel = scores + bias_ref[...]
    n = sel.shape[1]
    neg = -jnp.inf
    idx8 = lax.broadcasted_iota(jnp.int32, (EXPERTS_PER_GROUP, n), 0)
    grp_scores = []
    for g in range(N_EXPERT_GROUPS):
        sg = sel[g * EXPERTS_PER_GROUP:(g + 1) * EXPERTS_PER_GROUP]
        m1 = jnp.max(sg, axis=0, keepdims=True)
        first = jnp.min(jnp.where(sg == m1, idx8, EXPERTS_PER_GROUP), axis=0, keepdims=True)
        m2 = jnp.max(jnp.where(idx8 == first, neg, sg), axis=0, keepdims=True)
        grp_scores.append(m1 + m2)
    masked = []
    for g in range(N_EXPERT_GROUPS):
        rank = jnp.zeros((1, n), jnp.int32)
        for o in range(N_EXPERT_GROUPS):
            if o == g:
                continue
            ahead = (grp_scores[o] >= grp_scores[g]) if o < g else (grp_scores[o] > grp_scores[g])
            rank = rank + ahead.astype(jnp.int32)
        sg = sel[g * EXPERTS_PER_GROUP:(g + 1) * EXPERTS_PER_GROUP]
        masked.append(jnp.where(rank < TOPK_GROUPS, sg, neg))
    vals = jnp.concatenate(masked, axis=0)
    idx = lax.broadcasted_iota(jnp.int32, (N_EXPERTS, n), 0)
    picked = jnp.zeros((N_EXPERTS, n), F32)
    for _ in range(TOP_K):
        m = jnp.max(vals, axis=0, keepdims=True)
        first = jnp.min(jnp.where(vals == m, idx, N_EXPERTS), axis=0, keepdims=True)
        hit = idx == first
        picked = jnp.where(hit, scores, picked)
        vals = jnp.where(hit, neg, vals)
    gates = picked / jnp.sum(picked, axis=0, keepdims=True) * ROUTED_SCALE
    gate_ref[...] = gates
    cnt_ref[0] = jnp.sum(jnp.where(gates > 0.0, 1.0, 0.0), axis=1, keepdims=True)


def _route_call(logits_t, bias_col, ts):
    n_tok = logits_t.shape[1]
    nt = n_tok // ts
    return pl.pallas_call(
        _route_kernel,
        grid=(nt,),
        in_specs=[pl.BlockSpec((N_EXPERTS, ts), lambda i: (0, i)),
                  pl.BlockSpec((N_EXPERTS, 1), lambda i: (0, 0))],
        out_specs=[pl.BlockSpec((N_EXPERTS, ts), lambda i: (0, i)),
                   pl.BlockSpec((1, N_EXPERTS, 1), lambda i: (i, 0, 0))],
        out_shape=[jax.ShapeDtypeStruct((N_EXPERTS, n_tok), F32),
                   jax.ShapeDtypeStruct((nt, N_EXPERTS, 1), F32)],
        compiler_params=pltpu.CompilerParams(dimension_semantics=("arbitrary",),
                                             vmem_limit_bytes=VMEM_LIMIT),
    )(logits_t, bias_col)


RUN_ALIGN = 16
ROW_BLOCK = 512


def _run_capacity(ts):
    worst = ts * TOP_K + N_EXPERTS * (RUN_ALIGN - 1)
    return -(-worst // ROW_BLOCK) * ROW_BLOCK


def _for_each_run(cpad_sm, dst_sm, tile, fn):
    def body(e, off):
        n = pl.multiple_of(cpad_sm[tile * N_EXPERTS + e], RUN_ALIGN)
        d = pl.multiple_of(dst_sm[tile * N_EXPERTS + e], RUN_ALIGN)
        o = pl.multiple_of(off, RUN_ALIGN)

        @pl.when(n > 0)
        def _():
            fn(o, d, n)
        return off + n
    lax.fori_loop(0, N_EXPERTS, body, 0)


def _dispatch_kernel(cpad_sm, dst_sm, rtot_sm, tail_off_sm, tail_n_sm,
                     h2_ref, gt_ref, loff_ref, cnt_ref, xs_hbm, stage, zbuf, sems, zsem, *, ts):
    t = pl.program_id(0)
    nt = pl.num_programs(0)
    slot = t % 2

    def run_copy(slot_):
        return lambda o, d, n: pltpu.make_async_copy(
            stage.at[slot_, pl.ds(o, n)], xs_hbm.at[pl.ds(d, n)], sems.at[slot_])

    def tail_copies(fn):
        def body(e, c):
            n = pl.multiple_of(tail_n_sm[e], RUN_ALIGN)
            d = pl.multiple_of(tail_off_sm[e], RUN_ALIGN)

            @pl.when(n > 0)
            def _():
                fn(pltpu.make_async_copy(zbuf.at[pl.ds(0, n)], xs_hbm.at[pl.ds(d, n)], zsem))
            return c
        lax.fori_loop(0, N_EXPERTS, body, 0)

    @pl.when(t == 0)
    def _():
        zbuf[...] = jnp.zeros_like(zbuf)
        tail_copies(lambda c: c.start())

    @pl.when(t >= 2)
    def _():
        _for_each_run(cpad_sm, dst_sm, t - 2, lambda o, d, n: run_copy(slot)(o, d, n).wait())

    gt = gt_ref[...]
    chosen = gt > 0.0
    ii = lax.broadcasted_iota(jnp.int32, (ts, ts), 0)
    jj = lax.broadcasted_iota(jnp.int32, (ts, ts), 1)
    before = jnp.where(ii < jj, 1.0, 0.0).astype(BF16)
    prefix = _dot(jnp.where(chosen, 1.0, 0.0).astype(BF16), before)
    sp = jnp.where(chosen, prefix + 1.0, 0.0).astype(BF16)
    loff = loff_ref[0]
    cnt = cnt_ref[0]
    h2 = h2_ref[...]

    def block(k, c):
        r0 = pl.multiple_of(k * ROW_BLOCK, ROW_BLOCK)
        r = (lax.broadcasted_iota(jnp.int32, (ROW_BLOCK, N_EXPERTS), 0) + r0).astype(F32)
        own = jnp.where(r >= loff, jnp.where(r < loff + cnt, 1.0, 0.0), 0.0)
        want = r[:, :1] + 1.0 - jnp.sum(own * loff, axis=1, keepdims=True)
        v = _dot(own.astype(BF16), sp)
        onehot = jnp.where(v == want, 1.0, 0.0).astype(BF16)
        stage[slot, pl.ds(r0, ROW_BLOCK), :] = _dot(onehot, h2).astype(BF16)
        return c

    lax.fori_loop(0, (rtot_sm[t] + ROW_BLOCK - 1) // ROW_BLOCK, block, 0)
    _for_each_run(cpad_sm, dst_sm, t, lambda o, d, n: run_copy(slot)(o, d, n).start())

    @pl.when(t == nt - 1)
    def _():
        _for_each_run(cpad_sm, dst_sm, t, lambda o, d, n: run_copy(slot)(o, d, n).wait())

        @pl.when(t >= 1)
        def _():
            _for_each_run(cpad_sm, dst_sm, t - 1,
                          lambda o, d, n: run_copy(1 - slot)(o, d, n).wait())
        tail_copies(lambda c: c.wait())


def _dispatch_call(meta, h2, gates_t, ts, n_rows):
    n_tok = h2.shape[0]
    nt = n_tok // ts
    rcap = _run_capacity(ts)
    grid_spec = pltpu.PrefetchScalarGridSpec(
        num_scalar_prefetch=5, grid=(nt,),
        in_specs=[pl.BlockSpec((ts, D_MODEL), lambda t, *_: (t, 0)),
                  pl.BlockSpec((N_EXPERTS, ts), lambda t, *_: (0, t)),
                  pl.BlockSpec((1, 1, N_EXPERTS), lambda t, *_: (t, 0, 0)),
                  pl.BlockSpec((1, 1, N_EXPERTS), lambda t, *_: (t, 0, 0))],
        out_specs=pl.BlockSpec(memory_space=pl.ANY),
        scratch_shapes=[pltpu.VMEM((2, rcap, D_MODEL), BF16),
                        pltpu.VMEM((ROW_BLOCK, D_MODEL), BF16),
                        pltpu.SemaphoreType.DMA((2,)),
                        pltpu.SemaphoreType.DMA(())])
    return pl.pallas_call(
        functools.partial(_dispatch_kernel, ts=ts),
        grid_spec=grid_spec,
        out_shape=jax.ShapeDtypeStruct((n_rows, D_MODEL), BF16),
        compiler_params=pltpu.CompilerParams(dimension_semantics=("arbitrary",),
                                             vmem_limit_bytes=VMEM_LIMIT),
    )(meta["cpad"], meta["dst"], meta["rtot"], meta["tail_off"], meta["tail_n"],
      h2, gates_t, meta["loff_row"], meta["cnt_row"])


def _expert_kernel(bexp_sm, nvalid_sm, x_ref, wgu_ref, wd_ref, y_ref):
    @pl.when(pl.program_id(0) < nvalid_sm[0])
    def _():
        gu = _dot(x_ref[...], wgu_ref[0])
        hid = _silu(gu[:, :D_EXPERT]) * gu[:, D_EXPERT:]
        y_ref[...] = _dot(hid.astype(BF16), wd_ref[0]).astype(BF16)


def _expert_call(meta, xs, wgu, wd):
    n_rows = xs.shape[0]
    last = lambda b, nv: jnp.minimum(b, nv[0] - 1)
    grid_spec = pltpu.PrefetchScalarGridSpec(
        num_scalar_prefetch=2, grid=(n_rows // ROW_BLOCK,),
        in_specs=[pl.BlockSpec((ROW_BLOCK, D_MODEL), lambda b, be, nv: (last(b, nv), 0)),
                  pl.BlockSpec((1, D_MODEL, 2 * D_EXPERT), lambda b, be, nv: (be[last(b, nv)], 0, 0)),
                  pl.BlockSpec((1, D_EXPERT, D_MODEL), lambda b, be, nv: (be[last(b, nv)], 0, 0))],
        out_specs=pl.BlockSpec((ROW_BLOCK, D_MODEL), lambda b, be, nv: (last(b, nv), 0)))
    return pl.pallas_call(
        _expert_kernel,
        grid_spec=grid_spec,
        out_shape=jax.ShapeDtypeStruct((n_rows, D_MODEL), BF16),
        compiler_params=pltpu.CompilerParams(dimension_semantics=("arbitrary",),
                                             vmem_limit_bytes=VMEM_LIMIT),
    )(meta["bexp"], meta["nvalid"], xs, wgu, wd)


def _combine_kernel(cpad_sm, dst_sm, rtot_sm,
                    gk_ref, loff_ref, cnt_ref, h2_ref, x1_ref, mod_ref, sgu_ref, sd_ref, gfin_ref,
                    ys_hbm, o_ref, stage, sems, acc_ref, *, ts):
    t = pl.program_id(0)
    nt = pl.num_programs(0)
    slot = t % 2

    def run_copy(slot_):
        return lambda o, d, n: pltpu.make_async_copy(
            ys_hbm.at[pl.ds(d, n)], stage.at[slot_, pl.ds(o, n)], sems.at[slot_])

    @pl.when(t == 0)
    def _():
        stage[...] = jnp.zeros_like(stage)
        _for_each_run(cpad_sm, dst_sm, t, lambda o, d, n: run_copy(slot)(o, d, n).start())

    @pl.when(t + 1 < nt)
    def _():
        _for_each_run(cpad_sm, dst_sm, t + 1, lambda o, d, n: run_copy(1 - slot)(o, d, n).start())

    gk = gk_ref[...]
    chosen = gk > 0.0
    ii = lax.broadcasted_iota(jnp.int32, (ts, ts), 0)
    jj = lax.broadcasted_iota(jnp.int32, (ts, ts), 1)
    earlier = jnp.where(jj < ii, 1.0, 0.0).astype(BF16)
    prefix = _dot(earlier, jnp.where(chosen, 1.0, 0.0).astype(BF16))
    sp = jnp.where(chosen, prefix + 1.0, 0.0).astype(BF16)
    gkb = gk.astype(BF16)
    loff = loff_ref[0]
    cnt = cnt_ref[0]
    h2 = h2_ref[...]
    sgu = _dot(h2, sgu_ref[...])
    acc_ref[...] = _dot((_silu(sgu[:, :D_EXPERT]) * sgu[:, D_EXPERT:]).astype(BF16), sd_ref[...])

    _for_each_run(cpad_sm, dst_sm, t, lambda o, d, n: run_copy(slot)(o, d, n).wait())

    def block(k, c):
        r0 = pl.multiple_of(k * ROW_BLOCK, ROW_BLOCK)
        r = (lax.broadcasted_iota(jnp.int32, (N_EXPERTS, ROW_BLOCK), 1) + r0).astype(F32)
        own = jnp.where(r >= loff, jnp.where(r < loff + cnt, 1.0, 0.0), 0.0)
        want = r[:1] + 1.0 - jnp.sum(own * loff, axis=0, keepdims=True)
        ownb = own.astype(BF16)
        v = _dot(sp, ownb)
        w = jnp.where(v == want, _dot(gkb, ownb), 0.0).astype(BF16)
        acc_ref[...] += _dot(w, stage[slot, pl.ds(r0, ROW_BLOCK), :])
        return c

    lax.fori_loop(0, (rtot_sm[t] + ROW_BLOCK - 1) // ROW_BLOCK, block, 0)
    x2 = x1_ref[...] + mod_ref[0][5:6] * acc_ref[...]
    o_ref[...] = _rms(x2) * gfin_ref[...]


def _combine_call(meta, gates_k, h2, x1, modrows, sgu, sd, gfin, ys, ts, seq):
    n_tok = h2.shape[0]
    nt = n_tok // ts
    rcap = _run_capacity(ts)
    tiles_per_batch = seq // ts
    const = lambda shape: pl.BlockSpec(shape, lambda t, *_: (0,) * len(shape))
    grid_spec = pltpu.PrefetchScalarGridSpec(
        num_scalar_prefetch=3, grid=(nt,),
        in_specs=[pl.BlockSpec((ts, N_EXPERTS), lambda t, *_: (t, 0)),
                  pl.BlockSpec((1, N_EXPERTS, 1), lambda t, *_: (t, 0, 0)),
                  pl.BlockSpec((1, N_EXPERTS, 1), lambda t, *_: (t, 0, 0)),
                  pl.BlockSpec((ts, D_MODEL), lambda t, *_: (t, 0)),
                  pl.BlockSpec((ts, D_MODEL), lambda t, *_: (t, 0)),
                  pl.BlockSpec((1, SUBLANES, D_MODEL), lambda t, *_: (t // tiles_per_batch, 0, 0)),
                  const(sgu.shape), const(sd.shape), const(gfin.shape),
                  pl.BlockSpec(memory_space=pl.ANY)],
        out_specs=pl.BlockSpec((ts, D_MODEL), lambda t, *_: (t, 0)),
        scratch_shapes=[pltpu.VMEM((2, rcap, D_MODEL), BF16),
                        pltpu.SemaphoreType.DMA((2,)),
                        pltpu.VMEM((ts, D_MODEL), F32)])
    return pl.pallas_call(
        functools.partial(_combine_kernel, ts=ts),
        grid_spec=grid_spec,
        out_shape=jax.ShapeDtypeStruct((n_tok, D_MODEL), F32),
        compiler_params=pltpu.CompilerParams(dimension_semantics=("arbitrary",),
                                             vmem_limit_bytes=VMEM_LIMIT),
    )(meta["cpad"], meta["dst"], meta["rtot"], gates_k, meta["loff_col"], meta["cnt_col"],
      h2, x1, modrows, sgu, sd, gfin, ys)


def _dispatch_plan(cnt, ts, n_tok):
    nt = cnt.shape[0]
    cnt = cnt.astype(jnp.int32)
    cpad = (cnt + RUN_ALIGN - 1) // RUN_ALIGN * RUN_ALIGN
    loff = jnp.cumsum(cpad, axis=1) - cpad
    ctot = jnp.sum(cpad, axis=0)
    cblk = (ctot + ROW_BLOCK - 1) // ROW_BLOCK * ROW_BLOCK
    ends = jnp.cumsum(cblk)
    base = ends - cblk
    dst = base[None, :] + jnp.cumsum(cpad, axis=0) - cpad
    n_rows = n_tok * TOP_K + nt * N_EXPERTS * (RUN_ALIGN - 1) + N_EXPERTS * ROW_BLOCK
    n_rows = -(-n_rows // ROW_BLOCK) * ROW_BLOCK
    starts = jnp.arange(n_rows // ROW_BLOCK, dtype=jnp.int32) * ROW_BLOCK
    bexp = jnp.minimum(jnp.searchsorted(ends, starts, side="right"), N_EXPERTS - 1)
    meta = dict(cpad=cpad.reshape(-1), dst=dst.reshape(-1).astype(jnp.int32),
                rtot=jnp.sum(cpad, axis=1), tail_off=(base + ctot).astype(jnp.int32),
                tail_n=(cblk - ctot).astype(jnp.int32), bexp=bexp.astype(jnp.int32),
                nvalid=(ends[-1:] // ROW_BLOCK).astype(jnp.int32),
                loff_row=loff.astype(F32).reshape(nt, 1, N_EXPERTS),
                cnt_row=cnt.astype(F32).reshape(nt, 1, N_EXPERTS),
                loff_col=loff.astype(F32).reshape(nt, N_EXPERTS, 1),
                cnt_col=cnt.astype(F32).reshape(nt, N_EXPERTS, 1))
    return meta, n_rows


def _tile(n, pref):
    return pref if n % pref == 0 else n


def kernel(x, c, w_ada, b_ada, norm_mix_g, w_in, conv_a_w, ssm_conv_w, ssm_conv_b, ssm_dt_bias,
           ssm_a_log, ssm_d, ssm_norm_g, w_out, norm_ffn_g, w_router, router_bias, we_gate, we_up,
           we_down, ws_gate, ws_up, ws_down, norm_final_g):
    bsz, seq, _ = x.shape
    n_tok = bsz * seq
    assert w_ada.shape[0] == 1, "single-layer block"
    l = 0
    ts = _tile(seq, 256)
    n_main = 4 * D_CONV + D_XBC
    pad_heads = lambda a: jnp.pad(a.reshape(1, -1), ((0, 0), (0, LANES - SSM_HEADS)))
    mod = _ada_call(c, w_ada[l], b_ada[l])
    modrows = jnp.pad(jnp.transpose(mod, (1, 0, 2)), ((0, 0), (0, SUBLANES - 6), (0, 0)))
    w_main = w_in[l][:, :n_main].astype(BF16)
    w_dt = jnp.pad(w_in[l][:, n_main:], ((0, 0), (0, LANES - SSM_HEADS))).astype(BF16)
    x1, h2, logits_t = _mix_call(
        x, modrows, norm_mix_g[l].reshape(1, -1), norm_ffn_g[l].reshape(1, -1), w_main, w_dt,
        conv_a_w[l], ssm_conv_w[l], ssm_conv_b[l].reshape(1, -1), pad_heads(ssm_dt_bias[l]),
        pad_heads(-jnp.exp(ssm_a_log[l])), jnp.repeat(ssm_d[l], SSM_HEADDIM).reshape(1, -1),
        ssm_norm_g[l].reshape(1, -1), w_out[l][:D_CONV].astype(BF16),
        w_out[l][D_CONV:].astype(BF16), jnp.transpose(w_router[l]), ts)
    h2 = h2.reshape(n_tok, D_MODEL)
    gates_t, cnt = _route_call(logits_t, router_bias[l].reshape(-1, 1), ts)
    meta, n_rows = _dispatch_plan(cnt[:, :, 0], ts, n_tok)
    xs = _dispatch_call(meta, h2, gates_t, ts, n_rows)
    wgu = jnp.concatenate([we_gate[l], we_up[l]], axis=-1).astype(BF16)
    ys = _expert_call(meta, xs, wgu, we_down[l].astype(BF16))
    sgu = jnp.concatenate([ws_gate[l], ws_up[l]], axis=-1).astype(BF16)
    out = _combine_call(meta, jnp.transpose(gates_t), h2, x1.reshape(n_tok, D_MODEL), modrows,
                        sgu, ws_down[l].astype(BF16), norm_final_g.reshape(1, -1), ys, ts, seq)
    return out.reshape(bsz, seq, D_MODEL)
```

```python
import functools

import jax
import jax.numpy as jnp
from jax import lax
from jax.experimental import pallas as pl
from jax.experimental.pallas import tpu as pltpu

D_MODEL = 1024
D_CONV = 512
D_SSM = 512
SSM_HEADS = 8
SSM_HEADDIM = 64
SSM_GROUPS = 2
SSM_STATE = 128
HEADS_PER_GROUP = SSM_HEADS // SSM_GROUPS
GROUP_W = HEADS_PER_GROUP * SSM_HEADDIM
D_XBC = D_SSM + 2 * SSM_GROUPS * SSM_STATE
CHUNK = 64
N_EXPERTS = 64
N_EXPERT_GROUPS = 8
EXPERTS_PER_GROUP = N_EXPERTS // N_EXPERT_GROUPS
TOPK_GROUPS = 4
TOP_K = 8
D_EXPERT = 256
ROUTED_SCALE = 2.5
EPS = 1e-6
LANES = 128
SUBLANES = 8
VMEM_LIMIT = 56 * 1024 * 1024

F32 = jnp.float32
BF16 = jnp.bfloat16
HI = lax.Precision.HIGHEST


def _dot(a, b, precision=None):
    return jnp.dot(a, b, preferred_element_type=F32, precision=precision)


def _dot_nt(a, b, precision=None):
    return lax.dot_general(a, b, (((1,), (1,)), ((), ())), preferred_element_type=F32,
                           precision=precision)


def _dot_tn(a, b):
    return lax.dot_general(a, b, (((0,), (0,)), ((), ())), preferred_element_type=F32)


def _sigmoid(v):
    return 1.0 / (1.0 + jnp.exp(-v))


def _silu(v):
    return v * _sigmoid(v)


def _rms(v):
    return v * lax.rsqrt(jnp.mean(v * v, axis=-1, keepdims=True) + EPS)


def _ada_kernel(c_ref, w_ref, b_ref, o_ref):
    o_ref[0] = _dot(c_ref[...], w_ref[...], HI) + b_ref[...]


def _ada_call(c, w_ada, b_ada):
    bsz = c.shape[0]
    return pl.pallas_call(
        _ada_kernel,
        grid=(6,),
        in_specs=[pl.BlockSpec((bsz, D_MODEL), lambda j: (0, 0)),
                  pl.BlockSpec((D_MODEL, D_MODEL), lambda j: (0, j)),
                  pl.BlockSpec((1, D_MODEL), lambda j: (0, j))],
        out_specs=pl.BlockSpec((1, bsz, D_MODEL), lambda j: (j, 0, 0)),
        out_shape=jax.ShapeDtypeStruct((6, bsz, D_MODEL), F32),
        compiler_params=pltpu.CompilerParams(dimension_semantics=("arbitrary",),
                                             vmem_limit_bytes=VMEM_LIMIT),
    )(c, w_ada, b_ada.reshape(1, -1))


def _shift_rows(v, halo, k):
    rolled = pltpu.roll(v, k, 0)
    hrolled = pltpu.roll(halo, k, 0)
    rid = lax.broadcasted_iota(jnp.int32, (SUBLANES, v.shape[1]), 0)
    top = jnp.where(rid < k, hrolled, rolled[:SUBLANES])
    return jnp.concatenate([top, rolled[SUBLANES:]], axis=0)


def _mix_kernel(x_ref, mod_ref, gmix_ref, gffn_ref, win_ref, wdt_ref, cwa_ref, cws_ref, cbs_ref,
                dtb_ref, aneg_ref, dskip_ref, gssm_ref, wouta_ref, woutb_ref, wrt_ref,
                x1_ref, h2_ref, lg_ref,
                halo_v, halo_x, st_ref, xbc_s, dt_s, y_s, *, ts):
    s_idx = pl.program_id(1)

    @pl.when(s_idx == 0)
    def _():
        halo_v[...] = jnp.zeros_like(halo_v)
        halo_x[...] = jnp.zeros_like(halo_x)
        st_ref[...] = jnp.zeros_like(st_ref)

    x = x_ref[0]
    mod = mod_ref[0]
    h = _rms(x) * (gmix_ref[...] * (1.0 + mod[1:2])) + mod[0:1]
    hb = h.astype(BF16)

    def proj(lo, hi):
        return _dot(hb, win_ref[:, lo:hi])

    v = proj(D_CONV, 2 * D_CONV) * proj(2 * D_CONV, 3 * D_CONV)
    hv = halo_v[...]
    cwa = cwa_ref[...]
    conv_a = (cwa[2:3] * v + cwa[1:2] * _shift_rows(v, hv, 1) + cwa[0:1] * _shift_rows(v, hv, 2))
    halo_v[...] = v[ts - SUBLANES:]
    ya = (proj(0, D_CONV) * conv_a).astype(BF16)

    u = proj(4 * D_CONV, 4 * D_CONV + D_XBC)
    hx = halo_x[...]
    cws = cws_ref[...]
    conv_s = (cws[3:4] * u + cws[2:3] * _shift_rows(u, hx, 1) + cws[1:2] * _shift_rows(u, hx, 2)
              + cws[0:1] * _shift_rows(u, hx, 3)) + cbs_ref[...]
    halo_x[...] = u[ts - SUBLANES:]
    xbc_s[...] = _silu(conv_s)
    dt_raw = _dot(hb, wdt_ref[...]) + dtb_ref[...]
    dt_s[...] = jnp.maximum(dt_raw, 0.0) + jnp.log1p(jnp.exp(-jnp.abs(dt_raw)))

    li = lax.broadcasted_iota(jnp.int32, (CHUNK, D_SSM), 0)
    sj = lax.broadcasted_iota(jnp.int32, (CHUNK, D_SSM), 1) & (SSM_HEADDIM - 1)
    causal = sj <= li
    upto = (li <= sj).astype(F32)
    eh = lax.broadcasted_iota(jnp.int32, (LANES, D_SSM), 0)
    ej = lax.broadcasted_iota(jnp.int32, (LANES, D_SSM), 1) // SSM_HEADDIM
    expand = (eh == ej).astype(F32)
    tl = lax.broadcasted_iota(jnp.int32, (CHUNK, CHUNK), 0)
    tc = lax.broadcasted_iota(jnp.int32, (CHUNK, CHUNK), 1)
    tril = (tc <= tl).astype(F32)
    bi = lax.broadcasted_iota(jnp.int32, (GROUP_W, GROUP_W), 0) // SSM_HEADDIM
    bj = lax.broadcasted_iota(jnp.int32, (GROUP_W, GROUP_W), 1) // SSM_HEADDIM
    blockmask = (bi == bj).astype(F32)
    aneg = aneg_ref[...]
    dskip = dskip_ref[...]

    def chunk_body(c, carry):
        r0 = pl.multiple_of(c * CHUNK, CHUNK)
        xs = xbc_s[pl.ds(r0, CHUNK), 0:D_SSM]
        bm = xbc_s[pl.ds(r0, CHUNK), D_SSM:D_SSM + SSM_GROUPS * SSM_STATE]
        cm = xbc_s[pl.ds(r0, CHUNK), D_SSM + SSM_GROUPS * SSM_STATE:D_XBC]
        dt = dt_s[pl.ds(r0, CHUNK), :]
        a_exp = _dot(dt * aneg, expand, HI)
        dt_exp = _dot(dt, expand, HI)
        cs = _dot(tril, a_exp, HI)
        cs_row = jnp.sum(a_exp * upto, axis=0, keepdims=True)
        lmat = jnp.where(causal, jnp.exp(cs - cs_row), 0.0)
        dtx = xs * dt_exp
        out_decay = jnp.exp(cs)
        cs_last = cs[CHUNK - 1:CHUNK]
        end_decay = jnp.exp(cs_last - cs)
        chunk_decay = jnp.exp(cs_last)
        ys = []
        for g in range(SSM_GROUPS):
            lo, hi = g * GROUP_W, (g + 1) * GROUP_W
            bg = bm[:, g * SSM_STATE:(g + 1) * SSM_STATE].astype(BF16)
            cg = cm[:, g * SSM_STATE:(g + 1) * SSM_STATE].astype(BF16)
            brep = jnp.concatenate([bg] * HEADS_PER_GROUP, axis=0)
            w = (_dot_nt(cg, brep) * lmat[:, lo:hi]).astype(BF16)
            dtx_g = dtx[:, lo:hi]
            bd = (jnp.concatenate([dtx_g] * HEADS_PER_GROUP, axis=0) * blockmask).astype(BF16)
            st = st_ref[g]
            y_g = _dot(w, bd) + _dot(cg, st.astype(BF16)) * out_decay[:, lo:hi]
            new = _dot_tn(bg, (dtx_g * end_decay[:, lo:hi]).astype(BF16))
            st_ref[g] = st * chunk_decay[:, lo:hi] + new
            ys.append(y_g)
        y_s[pl.ds(r0, CHUNK), :] = jnp.concatenate(ys, axis=1) + dskip * xs
        return carry

    lax.fori_loop(0, ts // CHUNK, chunk_body, 0)

    z = proj(3 * D_CONV, 4 * D_CONV)
    yb = (_rms(y_s[...] * _silu(z)) * gssm_ref[...]).astype(BF16)

    out = _dot(ya, wouta_ref[...]) + _dot(yb, woutb_ref[...])
    x1 = x + mod[2:3] * out
    x1_ref[0] = x1
    h2 = _rms(x1) * (gffn_ref[...] * (1.0 + mod[4:5])) + mod[3:4]
    h2_ref[0] = h2.astype(BF16)
    lg_ref[...] = _dot_nt(wrt_ref[...], h2, HI)


def _mix_call(x, modrows, gmix, gffn, w_main, w_dt, cwa, cws, cbs, dtb, aneg, dskip, gssm,
              wout_a, wout_b, wr_t, ts):
    bsz, seq, _ = x.shape
    ns = seq // ts
    const = lambda shape: pl.BlockSpec(shape, lambda b, s: (0,) * len(shape))
    return pl.pallas_call(
        functools.partial(_mix_kernel, ts=ts),
        grid=(bsz, ns),
        in_specs=[pl.BlockSpec((1, ts, D_MODEL), lambda b, s: (b, s, 0)),
                  pl.BlockSpec((1, SUBLANES, D_MODEL), lambda b, s: (b, 0, 0)),
                  const(gmix.shape), const(gffn.shape), const(w_main.shape), const(w_dt.shape),
                  const(cwa.shape), const(cws.shape), const(cbs.shape), const(dtb.shape),
                  const(aneg.shape), const(dskip.shape), const(gssm.shape),
                  const(wout_a.shape), const(wout_b.shape), const(wr_t.shape)],
        out_specs=[pl.BlockSpec((1, ts, D_MODEL), lambda b, s: (b, s, 0)),
                   pl.BlockSpec((1, ts, D_MODEL), lambda b, s: (b, s, 0)),
                   pl.BlockSpec((N_EXPERTS, ts), lambda b, s: (0, b * ns + s))],
        out_shape=[jax.ShapeDtypeStruct((bsz, seq, D_MODEL), F32),
                   jax.ShapeDtypeStruct((bsz, seq, D_MODEL), BF16),
                   jax.ShapeDtypeStruct((N_EXPERTS, bsz * seq), F32)],
        scratch_shapes=[pltpu.VMEM((SUBLANES, D_CONV), F32),
                        pltpu.VMEM((SUBLANES, D_XBC), F32),
                        pltpu.VMEM((SSM_GROUPS, SSM_STATE, GROUP_W), F32),
                        pltpu.VMEM((ts, D_XBC), F32),
                        pltpu.VMEM((ts, LANES), F32),
                        pltpu.VMEM((ts, D_SSM), F32)],
        compiler_params=pltpu.CompilerParams(dimension_semantics=("arbitrary", "arbitrary"),
                                             vmem_limit_bytes=VMEM_LIMIT),
    )(x, modrows, gmix, gffn, w_main, w_dt, cwa, cws, cbs, dtb, aneg, dskip, gssm,
      wout_a, wout_b, wr_t)


def _route_kernel(lg_ref, bias_ref, gate_ref, cnt_ref):
    scores = _sigmoid(lg_ref[...])
    sel = scores + bias_ref[...]
    n = sel.shape[1]
    neg = -jnp.inf
    idx8 = lax.broadcasted_iota(jnp.int32, (EXPERTS_PER_GROUP, n), 0)
    grp_scores = []
    for g in range(N_EXPERT_GROUPS):
        sg = sel[g * EXPERTS_PER_GROUP:(g + 1) * EXPERTS_PER_GROUP]
        m1 = jnp.max(sg, axis=0, keepdims=True)
        first = jnp.min(jnp.where(sg == m1, idx8, EXPERTS_PER_GROUP), axis=0, keepdims=True)
        m2 = jnp.max(jnp.where(idx8 == first, neg, sg), axis=0, keepdims=True)
        grp_scores.append(m1 + m2)
    masked = []
    for g in range(N_EXPERT_GROUPS):
        rank = jnp.zeros((1, n), jnp.int32)
        for o in range(N_EXPERT_GROUPS):
            if o == g:
                continue
            ahead = (grp_scores[o] >= grp_scores[g]) if o < g else (grp_scores[o] > grp_scores[g])
            rank = rank + ahead.astype(jnp.int32)
        sg = sel[g * EXPERTS_PER_GROUP:(g + 1) * EXPERTS_PER_GROUP]
        masked.append(jnp.where(rank < TOPK_GROUPS, sg, neg))
    vals = jnp.concatenate(masked, axis=0)
    idx = lax.broadcasted_iota(jnp.int32, (N_EXPERTS, n), 0)
    picked = jnp.zeros((N_EXPERTS, n), F32)
    for _ in range(TOP_K):
        m = jnp.max(vals, axis=0, keepdims=True)
        first = jnp.min(jnp.where(vals == m, idx, N_EXPERTS), axis=0, keepdims=True)
        hit = idx == first
        picked = jnp.where(hit, scores, picked)
        vals = jnp.where(hit, neg, vals)
    gates = picked / jnp.sum(picked, axis=0, keepdims=True) * ROUTED_SCALE
    gate_ref[...] = gates
    cnt_ref[0] = jnp.sum(jnp.where(gates > 0.0, 1.0, 0.0), axis=1, keepdims=True)


def _route_call(logits_t, bias_col, ts):
    n_tok = logits_t.shape[1]
    nt = n_tok // ts
    return pl.pallas_call(
        _route_kernel,
        grid=(nt,),
        in_specs=[pl.BlockSpec((N_EXPERTS, ts), lambda i: (0, i)),
                  pl.BlockSpec((N_EXPERTS, 1), lambda i: (0, 0))],
        out_specs=[pl.BlockSpec((N_EXPERTS, ts), lambda i: (0, i)),
                   pl.BlockSpec((1, N_EXPERTS, 1), lambda i: (i, 0, 0))],
        out_shape=[jax.ShapeDtypeStruct((N_EXPERTS, n_tok), F32),
                   jax.ShapeDtypeStruct((nt, N_EXPERTS, 1), F32)],
        compiler_params=pltpu.CompilerParams(dimension_semantics=("arbitrary",),
                                             vmem_limit_bytes=VMEM_LIMIT),
    )(logits_t, bias_col)


RUN_ALIGN = 16
ROW_BLOCK = 512
EXPERT_BLOCK = 1024
RUN_UNROLL = 8


def _run_capacity(ts):
    worst = ts * TOP_K + N_EXPERTS * (RUN_ALIGN - 1)
    return -(-worst // ROW_BLOCK) * ROW_BLOCK


def _for_each_run(cpad_sm, dst_sm, tile, fn):
    def body(e, off):
        n = pl.multiple_of(cpad_sm[tile * N_EXPERTS + e], RUN_ALIGN)
        d = pl.multiple_of(dst_sm[tile * N_EXPERTS + e], RUN_ALIGN)
        o = pl.multiple_of(off, RUN_ALIGN)

        @pl.when(n > 0)
        def _():
            fn(o, d, n)
        return off + n
    lax.fori_loop(0, N_EXPERTS, body, 0, unroll=RUN_UNROLL)


def _dispatch_kernel(cpad_sm, dst_sm, rtot_sm, tail_off_sm, tail_n_sm,
                     h2_ref, gt_ref, loff_ref, cnt_ref, xs_hbm, stage, zbuf, sems, zsem, *, ts):
    t = pl.program_id(0)
    nt = pl.num_programs(0)
    slot = t % 2

    def run_copy(slot_):
        return lambda o, d, n: pltpu.make_async_copy(
            stage.at[slot_, pl.ds(o, n)], xs_hbm.at[pl.ds(d, n)], sems.at[slot_])

    def tail_copies(fn):
        def body(e, c):
            n = pl.multiple_of(tail_n_sm[e], RUN_ALIGN)
            d = pl.multiple_of(tail_off_sm[e], RUN_ALIGN)

            @pl.when(n > 0)
            def _():
                fn(pltpu.make_async_copy(zbuf.at[pl.ds(0, n)], xs_hbm.at[pl.ds(d, n)], zsem))
            return c
        lax.fori_loop(0, N_EXPERTS, body, 0)

    @pl.when(t == 0)
    def _():
        zbuf[...] = jnp.zeros_like(zbuf)
        tail_copies(lambda c: c.start())

    @pl.when(t >= 2)
    def _():
        _for_each_run(cpad_sm, dst_sm, t - 2, lambda o, d, n: run_copy(slot)(o, d, n).wait())

    gt = gt_ref[...]
    chosen = gt > 0.0
    ii = lax.broadcasted_iota(jnp.int32, (ts, ts), 0)
    jj = lax.broadcasted_iota(jnp.int32, (ts, ts), 1)
    before = jnp.where(ii < jj, 1.0, 0.0).astype(BF16)
    prefix = _dot(jnp.where(chosen, 1.0, 0.0).astype(BF16), before)
    sp = jnp.where(chosen, prefix + 1.0, 0.0).astype(BF16)
    loff = loff_ref[0]
    cnt = cnt_ref[0]
    h2 = h2_ref[...]

    def block(k, c):
        r0 = pl.multiple_of(k * ROW_BLOCK, ROW_BLOCK)
        r = (lax.broadcasted_iota(jnp.int32, (ROW_BLOCK, N_EXPERTS), 0) + r0).astype(F32)
        own = jnp.where(r >= loff, jnp.where(r < loff + cnt, 1.0, 0.0), 0.0)
        want = r[:, :1] + 1.0 - jnp.sum(own * loff, axis=1, keepdims=True)
        v = _dot(own.astype(BF16), sp)
        onehot = jnp.where(v == want, 1.0, 0.0).astype(BF16)
        stage[slot, pl.ds(r0, ROW_BLOCK), :] = _dot(onehot, h2).astype(BF16)
        return c

    lax.fori_loop(0, (rtot_sm[t] + ROW_BLOCK - 1) // ROW_BLOCK, block, 0)
    _for_each_run(cpad_sm, dst_sm, t, lambda o, d, n: run_copy(slot)(o, d, n).start())

    @pl.when(t == nt - 1)
    def _():
        _for_each_run(cpad_sm, dst_sm, t, lambda o, d, n: run_copy(slot)(o, d, n).wait())

        @pl.when(t >= 1)
        def _():
            _for_each_run(cpad_sm, dst_sm, t - 1,
                          lambda o, d, n: run_copy(1 - slot)(o, d, n).wait())
        tail_copies(lambda c: c.wait())


def _dispatch_call(meta, h2, gates_t, ts, n_rows):
    n_tok = h2.shape[0]
    nt = n_tok // ts
    rcap = _run_capacity(ts)
    grid_spec = pltpu.PrefetchScalarGridSpec(
        num_scalar_prefetch=5, grid=(nt,),
        in_specs=[pl.BlockSpec((ts, D_MODEL), lambda t, *_: (t, 0)),
                  pl.BlockSpec((N_EXPERTS, ts), lambda t, *_: (0, t)),
                  pl.BlockSpec((1, 1, N_EXPERTS), lambda t, *_: (t, 0, 0)),
                  pl.BlockSpec((1, 1, N_EXPERTS), lambda t, *_: (t, 0, 0))],
        out_specs=pl.BlockSpec(memory_space=pl.ANY),
        scratch_shapes=[pltpu.VMEM((2, rcap, D_MODEL), BF16),
                        pltpu.VMEM((EXPERT_BLOCK, D_MODEL), BF16),
                        pltpu.SemaphoreType.DMA((2,)),
                        pltpu.SemaphoreType.DMA(())])
    return pl.pallas_call(
        functools.partial(_dispatch_kernel, ts=ts),
        grid_spec=grid_spec,
        out_shape=jax.ShapeDtypeStruct((n_rows, D_MODEL), BF16),
        compiler_params=pltpu.CompilerParams(dimension_semantics=("arbitrary",),
                                             vmem_limit_bytes=VMEM_LIMIT),
    )(meta["cpad"], meta["dst"], meta["rtot"], meta["tail_off"], meta["tail_n"],
      h2, gates_t, meta["loff_row"], meta["cnt_row"])


def _expert_kernel(bexp_sm, nvalid_sm, x_ref, wgu_ref, wd_ref, y_ref):
    @pl.when(pl.program_id(0) < nvalid_sm[0])
    def _():
        gu = _dot(x_ref[...], wgu_ref[0])
        hid = _silu(gu[:, :D_EXPERT]) * gu[:, D_EXPERT:]
        y_ref[...] = _dot(hid.astype(BF16), wd_ref[0]).astype(BF16)


def _expert_call(meta, xs, wgu, wd):
    n_rows = xs.shape[0]
    last = lambda b, nv: jnp.minimum(b, nv[0] - 1)
    grid_spec = pltpu.PrefetchScalarGridSpec(
        num_scalar_prefetch=2, grid=(n_rows // EXPERT_BLOCK,),
        in_specs=[pl.BlockSpec((EXPERT_BLOCK, D_MODEL), lambda b, be, nv: (last(b, nv), 0)),
                  pl.BlockSpec((1, D_MODEL, 2 * D_EXPERT), lambda b, be, nv: (be[last(b, nv)], 0, 0)),
                  pl.BlockSpec((1, D_EXPERT, D_MODEL), lambda b, be, nv: (be[last(b, nv)], 0, 0))],
        out_specs=pl.BlockSpec((EXPERT_BLOCK, D_MODEL), lambda b, be, nv: (last(b, nv), 0)))
    return pl.pallas_call(
        _expert_kernel,
        grid_spec=grid_spec,
        out_shape=jax.ShapeDtypeStruct((n_rows, D_MODEL), BF16),
        compiler_params=pltpu.CompilerParams(dimension_semantics=("arbitrary",),
                                             vmem_limit_bytes=VMEM_LIMIT),
    )(meta["bexp"], meta["nvalid"], xs, wgu, wd)


def _combine_kernel(cpad_sm, dst_sm, rtot_sm,
                    gk_ref, loff_ref, cnt_ref, h2_ref, x1_ref, mod_ref, sgu_ref, sd_ref, gfin_ref,
                    ys_hbm, o_ref, stage, sems, acc_ref, *, ts):
    t = pl.program_id(0)
    nt = pl.num_programs(0)
    slot = t % 2

    def run_copy(slot_):
        return lambda o, d, n: pltpu.make_async_copy(
            ys_hbm.at[pl.ds(d, n)], stage.at[slot_, pl.ds(o, n)], sems.at[slot_])

    @pl.when(t == 0)
    def _():
        stage[...] = jnp.zeros_like(stage)
        _for_each_run(cpad_sm, dst_sm, t, lambda o, d, n: run_copy(slot)(o, d, n).start())

    @pl.when(t + 1 < nt)
    def _():
        _for_each_run(cpad_sm, dst_sm, t + 1, lambda o, d, n: run_copy(1 - slot)(o, d, n).start())

    gk = gk_ref[...]
    chosen = gk > 0.0
    ii = lax.broadcasted_iota(jnp.int32, (ts, ts), 0)
    jj = lax.broadcasted_iota(jnp.int32, (ts, ts), 1)
    earlier = jnp.where(jj < ii, 1.0, 0.0).astype(BF16)
    prefix = _dot(earlier, jnp.where(chosen, 1.0, 0.0).astype(BF16))
    sp = jnp.where(chosen, prefix + 1.0, 0.0).astype(BF16)
    gkb = gk.astype(BF16)
    loff = loff_ref[0]
    cnt = cnt_ref[0]
    h2 = h2_ref[...]
    sgu = _dot(h2, sgu_ref[...])
    acc_ref[...] = _dot((_silu(sgu[:, :D_EXPERT]) * sgu[:, D_EXPERT:]).astype(BF16), sd_ref[...])

    _for_each_run(cpad_sm, dst_sm, t, lambda o, d, n: run_copy(slot)(o, d, n).wait())

    def block(k, c):
        r0 = pl.multiple_of(k * ROW_BLOCK, ROW_BLOCK)
        r = (lax.broadcasted_iota(jnp.int32, (N_EXPERTS, ROW_BLOCK), 1) + r0).astype(F32)
        own = jnp.where(r >= loff, jnp.where(r < loff + cnt, 1.0, 0.0), 0.0)
        want = r[:1] + 1.0 - jnp.sum(own * loff, axis=0, keepdims=True)
        ownb = own.astype(BF16)
        v = _dot(sp, ownb)
        w = jnp.where(v == want, _dot(gkb, ownb), 0.0).astype(BF16)
        acc_ref[...] += _dot(w, stage[slot, pl.ds(r0, ROW_BLOCK), :])
        return c

    lax.fori_loop(0, (rtot_sm[t] + ROW_BLOCK - 1) // ROW_BLOCK, block, 0)
    x2 = x1_ref[...] + mod_ref[0][5:6] * acc_ref[...]
    o_ref[...] = _rms(x2) * gfin_ref[...]


def _combine_call(meta, gates_k, h2, x1, modrows, sgu, sd, gfin, ys, ts, seq):
    n_tok = h2.shape[0]
    nt = n_tok // ts
    rcap = _run_capacity(ts)
    tiles_per_batch = seq // ts
    const = lambda shape: pl.BlockSpec(shape, lambda t, *_: (0,) * len(shape))
    grid_spec = pltpu.PrefetchScalarGridSpec(
        num_scalar_prefetch=3, grid=(nt,),
        in_specs=[pl.BlockSpec((ts, N_EXPERTS), lambda t, *_: (t, 0)),
                  pl.BlockSpec((1, N_EXPERTS, 1), lambda t, *_: (t, 0, 0)),
                  pl.BlockSpec((1, N_EXPERTS, 1), lambda t, *_: (t, 0, 0)),
                  pl.BlockSpec((ts, D_MODEL), lambda t, *_: (t, 0)),
                  pl.BlockSpec((ts, D_MODEL), lambda t, *_: (t, 0)),
                  pl.BlockSpec((1, SUBLANES, D_MODEL), lambda t, *_: (t // tiles_per_batch, 0, 0)),
                  const(sgu.shape), const(sd.shape), const(gfin.shape),
                  pl.BlockSpec(memory_space=pl.ANY)],
        out_specs=pl.BlockSpec((ts, D_MODEL), lambda t, *_: (t, 0)),
        scratch_shapes=[pltpu.VMEM((2, rcap, D_MODEL), BF16),
                        pltpu.SemaphoreType.DMA((2,)),
                        pltpu.VMEM((ts, D_MODEL), F32)])
    return pl.pallas_call(
        functools.partial(_combine_kernel, ts=ts),
        grid_spec=grid_spec,
        out_shape=jax.ShapeDtypeStruct((n_tok, D_MODEL), F32),
        compiler_params=pltpu.CompilerParams(dimension_semantics=("arbitrary",),
                                             vmem_limit_bytes=VMEM_LIMIT),
    )(meta["cpad"], meta["dst"], meta["rtot"], gates_k, meta["loff_col"], meta["cnt_col"],
      h2, x1, modrows, sgu, sd, gfin, ys)


def _dispatch_plan(cnt, ts, n_tok):
    nt = cnt.shape[0]
    cnt = cnt.astype(jnp.int32)
    cpad = (cnt + RUN_ALIGN - 1) // RUN_ALIGN * RUN_ALIGN
    loff = jnp.cumsum(cpad, axis=1) - cpad
    ctot = jnp.sum(cpad, axis=0)
    cblk = (ctot + EXPERT_BLOCK - 1) // EXPERT_BLOCK * EXPERT_BLOCK
    ends = jnp.cumsum(cblk)
    base = ends - cblk
    dst = base[None, :] + jnp.cumsum(cpad, axis=0) - cpad
    n_rows = n_tok * TOP_K + nt * N_EXPERTS * (RUN_ALIGN - 1) + N_EXPERTS * EXPERT_BLOCK
    n_rows = -(-n_rows // EXPERT_BLOCK) * EXPERT_BLOCK
    starts = jnp.arange(n_rows // EXPERT_BLOCK, dtype=jnp.int32) * EXPERT_BLOCK
    bexp = jnp.minimum(jnp.sum(starts[:, None] >= ends[None, :], axis=1), N_EXPERTS - 1)
    meta = dict(cpad=cpad.reshape(-1), dst=dst.reshape(-1).astype(jnp.int32),
                rtot=jnp.sum(cpad, axis=1), tail_off=(base + ctot).astype(jnp.int32),
                tail_n=(cblk - ctot).astype(jnp.int32), bexp=bexp.astype(jnp.int32),
                nvalid=(ends[-1:] // EXPERT_BLOCK).astype(jnp.int32),
                loff_row=loff.astype(F32).reshape(nt, 1, N_EXPERTS),
                cnt_row=cnt.astype(F32).reshape(nt, 1, N_EXPERTS),
                loff_col=loff.astype(F32).reshape(nt, N_EXPERTS, 1),
                cnt_col=cnt.astype(F32).reshape(nt, N_EXPERTS, 1))
    return meta, n_rows


def _tile(n, pref):
    return pref if n % pref == 0 else n


def kernel(x, c, w_ada, b_ada, norm_mix_g, w_in, conv_a_w, ssm_conv_w, ssm_conv_b, ssm_dt_bias,
           ssm_a_log, ssm_d, ssm_norm_g, w_out, norm_ffn_g, w_router, router_bias, we_gate, we_up,
           we_down, ws_gate, ws_up, ws_down, norm_final_g):
    bsz, seq, _ = x.shape
    n_tok = bsz * seq
    assert w_ada.shape[0] == 1, "single-layer block"
    l = 0
    ts = _tile(seq, 256)
    n_main = 4 * D_CONV + D_XBC
    pad_heads = lambda a: jnp.pad(a.reshape(1, -1), ((0, 0), (0, LANES - SSM_HEADS)))
    mod = _ada_call(c, w_ada[l], b_ada[l])
    modrows = jnp.pad(jnp.transpose(mod, (1, 0, 2)), ((0, 0), (0, SUBLANES - 6), (0, 0)))
    w_main = w_in[l][:, :n_main].astype(BF16)
    w_dt = jnp.pad(w_in[l][:, n_main:], ((0, 0), (0, LANES - SSM_HEADS))).astype(BF16)
    x1, h2, logits_t = _mix_call(
        x, modrows, norm_mix_g[l].reshape(1, -1), norm_ffn_g[l].reshape(1, -1), w_main, w_dt,
        conv_a_w[l], ssm_conv_w[l], ssm_conv_b[l].reshape(1, -1), pad_heads(ssm_dt_bias[l]),
        pad_heads(-jnp.exp(ssm_a_log[l])), jnp.repeat(ssm_d[l], SSM_HEADDIM).reshape(1, -1),
        ssm_norm_g[l].reshape(1, -1), w_out[l][:D_CONV].astype(BF16),
        w_out[l][D_CONV:].astype(BF16), jnp.transpose(w_router[l]), _tile(seq, 512))
    h2 = h2.reshape(n_tok, D_MODEL)
    gates_t, cnt = _route_call(logits_t, router_bias[l].reshape(-1, 1), ts)
    meta, n_rows = _dispatch_plan(cnt[:, :, 0], ts, n_tok)
    xs = _dispatch_call(meta, h2, gates_t, ts, n_rows)
    wgu = jnp.concatenate([we_gate[l], we_up[l]], axis=-1).astype(BF16)
    ys = _expert_call(meta, xs, wgu, we_down[l].astype(BF16))
    sgu = jnp.concatenate([ws_gate[l], ws_up[l]], axis=-1).astype(BF16)
    out = _combine_call(meta, jnp.transpose(gates_t), h2, x1.reshape(n_tok, D_MODEL), modrows,
                        sgu, ws_down[l].astype(BF16), norm_final_g.reshape(1, -1), ys, ts, seq)
    return out.reshape(bsz, seq, D_MODEL)
```

```python
import functools

import jax
import jax.numpy as jnp
from jax import lax
from jax.experimental import pallas as pl
from jax.experimental.pallas import tpu as pltpu

D_MODEL = 1024
D_CONV = 512
D_SSM = 512
SSM_HEADS = 8
SSM_HEADDIM = 64
SSM_GROUPS = 2
SSM_STATE = 128
HEADS_PER_GROUP = SSM_HEADS // SSM_GROUPS
GROUP_W = HEADS_PER_GROUP * SSM_HEADDIM
D_XBC = D_SSM + 2 * SSM_GROUPS * SSM_STATE
CHUNK = 64
N_EXPERTS = 64
N_EXPERT_GROUPS = 8
EXPERTS_PER_GROUP = N_EXPERTS // N_EXPERT_GROUPS
TOPK_GROUPS = 4
TOP_K = 8
D_EXPERT = 256
ROUTED_SCALE = 2.5
EPS = 1e-6
LANES = 128
SUBLANES = 8
VMEM_LIMIT = 56 * 1024 * 1024

F32 = jnp.float32
BF16 = jnp.bfloat16
HI = lax.Precision.HIGHEST


def _dot(a, b, precision=None):
    return jnp.dot(a, b, preferred_element_type=F32, precision=precision)


def _dot_nt(a, b, precision=None):
    return lax.dot_general(a, b, (((1,), (1,)), ((), ())), preferred_element_type=F32,
                           precision=precision)


def _dot_tn(a, b):
    return lax.dot_general(a, b, (((0,), (0,)), ((), ())), preferred_element_type=F32)


def _sigmoid(v):
    return 1.0 / (1.0 + jnp.exp(-v))


def _silu(v):
    return v * _sigmoid(v)


def _rms(v):
    return v * lax.rsqrt(jnp.mean(v * v, axis=-1, keepdims=True) + EPS)


def _split3(v):
    a = v.astype(BF16)
    r = v - a.astype(F32)
    b = r.astype(BF16)
    return a, b, (r - b.astype(F32)).astype(BF16)


def _ada_kernel(c_ref, w_ref, b_ref, o_ref):
    o_ref[0] = _dot(c_ref[...], w_ref[...], HI) + b_ref[...]


def _ada_call(c, w_ada, b_ada):
    bsz = c.shape[0]
    return pl.pallas_call(
        _ada_kernel,
        grid=(6,),
        in_specs=[pl.BlockSpec((bsz, D_MODEL), lambda j: (0, 0)),
                  pl.BlockSpec((D_MODEL, D_MODEL), lambda j: (0, j)),
                  pl.BlockSpec((1, D_MODEL), lambda j: (0, j))],
        out_specs=pl.BlockSpec((1, bsz, D_MODEL), lambda j: (j, 0, 0)),
        out_shape=jax.ShapeDtypeStruct((6, bsz, D_MODEL), F32),
        compiler_params=pltpu.CompilerParams(dimension_semantics=("arbitrary",),
                                             vmem_limit_bytes=VMEM_LIMIT),
    )(c, w_ada, b_ada.reshape(1, -1))


def _shift_rows(v, halo, k):
    rolled = pltpu.roll(v, k, 0)
    hrolled = pltpu.roll(halo, k, 0)
    rid = lax.broadcasted_iota(jnp.int32, (SUBLANES, v.shape[1]), 0)
    top = jnp.where(rid < k, hrolled, rolled[:SUBLANES])
    return jnp.concatenate([top, rolled[SUBLANES:]], axis=0)


def _mix_kernel(x_ref, mod_ref, gmix_ref, gffn_ref, win_ref, wdt_ref, cwa_ref, cws_ref, cbs_ref,
                dtb_ref, aneg_ref, dskip_ref, gssm_ref, wouta_ref, woutb_ref, wrt_ref,
                x1_ref, h2_ref, lg_ref,
                halo_v, halo_x, st_ref, xbc_s, dt_s, y_s, *, ts):
    s_idx = pl.program_id(1)

    @pl.when(s_idx == 0)
    def _():
        halo_v[...] = jnp.zeros_like(halo_v)
        halo_x[...] = jnp.zeros_like(halo_x)
        st_ref[...] = jnp.zeros_like(st_ref)

    x = x_ref[0]
    mod = mod_ref[0]
    h = _rms(x) * (gmix_ref[...] * (1.0 + mod[1:2])) + mod[0:1]
    hb = h.astype(BF16)

    def proj(lo, hi):
        return _dot(hb, win_ref[:, lo:hi])

    v = proj(D_CONV, 2 * D_CONV) * proj(2 * D_CONV, 3 * D_CONV)
    hv = halo_v[...]
    cwa = cwa_ref[...]
    conv_a = (cwa[2:3] * v + cwa[1:2] * _shift_rows(v, hv, 1) + cwa[0:1] * _shift_rows(v, hv, 2))
    halo_v[...] = v[ts - SUBLANES:]
    ya = (proj(0, D_CONV) * conv_a).astype(BF16)

    u = proj(4 * D_CONV, 4 * D_CONV + D_XBC)
    hx = halo_x[...]
    cws = cws_ref[...]
    conv_s = (cws[3:4] * u + cws[2:3] * _shift_rows(u, hx, 1) + cws[1:2] * _shift_rows(u, hx, 2)
              + cws[0:1] * _shift_rows(u, hx, 3)) + cbs_ref[...]
    halo_x[...] = u[ts - SUBLANES:]
    xbc_s[...] = _silu(conv_s)
    dt_raw = _dot(hb, wdt_ref[...]) + dtb_ref[...]
    dt_s[...] = jnp.maximum(dt_raw, 0.0) + jnp.log1p(jnp.exp(-jnp.abs(dt_raw)))

    li = lax.broadcasted_iota(jnp.int32, (CHUNK, D_SSM), 0)
    sj = lax.broadcasted_iota(jnp.int32, (CHUNK, D_SSM), 1) & (SSM_HEADDIM - 1)
    causal = sj <= li
    upto = (li <= sj).astype(F32)
    eh = lax.broadcasted_iota(jnp.int32, (LANES, D_SSM), 0)
    ej = lax.broadcasted_iota(jnp.int32, (LANES, D_SSM), 1) // SSM_HEADDIM
    expand = jnp.where(eh == ej, 1.0, 0.0).astype(BF16)
    expand3 = jnp.concatenate([expand] * 3, axis=0)
    tl = lax.broadcasted_iota(jnp.int32, (CHUNK, 3 * CHUNK), 0)
    tc = lax.broadcasted_iota(jnp.int32, (CHUNK, 3 * CHUNK), 1) & (CHUNK - 1)
    tril3 = jnp.where(tc <= tl, 1.0, 0.0).astype(BF16)
    bi = lax.broadcasted_iota(jnp.int32, (GROUP_W, GROUP_W), 0) // SSM_HEADDIM
    bj = lax.broadcasted_iota(jnp.int32, (GROUP_W, GROUP_W), 1) // SSM_HEADDIM
    blockmask = (bi == bj).astype(F32)
    aneg = aneg_ref[...]
    dskip = dskip_ref[...]

    def chunk_body(c, carry):
        r0 = pl.multiple_of(c * CHUNK, CHUNK)
        xs = xbc_s[pl.ds(r0, CHUNK), 0:D_SSM]
        bm = xbc_s[pl.ds(r0, CHUNK), D_SSM:D_SSM + SSM_GROUPS * SSM_STATE]
        cm = xbc_s[pl.ds(r0, CHUNK), D_SSM + SSM_GROUPS * SSM_STATE:D_XBC]
        dt = dt_s[pl.ds(r0, CHUNK), :]
        both = _dot(jnp.concatenate(_split3(jnp.concatenate([dt * aneg, dt], axis=0)), axis=1),
                    expand3)
        a_exp = both[:CHUNK]
        dt_exp = both[CHUNK:]
        cs = _dot(tril3, jnp.concatenate(_split3(a_exp), axis=0))
        cs_row = jnp.sum(a_exp * upto, axis=0, keepdims=True)
        lmat = jnp.where(causal, jnp.exp(cs - cs_row), 0.0)
        dtx = xs * dt_exp
        out_decay = jnp.exp(cs)
        cs_last = cs[CHUNK - 1:CHUNK]
        end_decay = jnp.exp(cs_last - cs)
        chunk_decay = jnp.exp(cs_last)
        ys = []
        for g in range(SSM_GROUPS):
            lo, hi = g * GROUP_W, (g + 1) * GROUP_W
            bg = bm[:, g * SSM_STATE:(g + 1) * SSM_STATE].astype(BF16)
            cg = cm[:, g * SSM_STATE:(g + 1) * SSM_STATE].astype(BF16)
            brep = jnp.concatenate([bg] * HEADS_PER_GROUP, axis=0)
            w = (_dot_nt(cg, brep) * lmat[:, lo:hi]).astype(BF16)
            dtx_g = dtx[:, lo:hi]
            bd = (jnp.concatenate([dtx_g] * HEADS_PER_GROUP, axis=0) * blockmask).astype(BF16)
            st = st_ref[g]
            y_g = _dot(w, bd) + _dot(cg, st.astype(BF16)) * out_decay[:, lo:hi]
            new = _dot_tn(bg, (dtx_g * end_decay[:, lo:hi]).astype(BF16))
            st_ref[g] = st * chunk_decay[:, lo:hi] + new
            ys.append(y_g)
        y_s[pl.ds(r0, CHUNK), :] = jnp.concatenate(ys, axis=1) + dskip * xs
        return carry

    lax.fori_loop(0, ts // CHUNK, chunk_body, 0, unroll=True)

    z = proj(3 * D_CONV, 4 * D_CONV)
    yb = (_rms(y_s[...] * _silu(z)) * gssm_ref[...]).astype(BF16)

    out = _dot(ya, wouta_ref[...]) + _dot(yb, woutb_ref[...])
    x1 = x + mod[2:3] * out
    x1_ref[0] = x1
    h2 = _rms(x1) * (gffn_ref[...] * (1.0 + mod[4:5])) + mod[3:4]
    h2_hi = h2.astype(BF16)
    h2_ref[0] = h2_hi
    h2_lo = (h2 - h2_hi.astype(F32)).astype(BF16)
    w_hi = wrt_ref[0]
    lg_ref[...] = _dot_nt(w_hi, h2_hi) + (_dot_nt(w_hi, h2_lo) + _dot_nt(wrt_ref[1], h2_hi))


def _mix_call(x, modrows, gmix, gffn, w_main, w_dt, cwa, cws, cbs, dtb, aneg, dskip, gssm,
              wout_a, wout_b, wr_t, ts):
    bsz, seq, _ = x.shape
    ns = seq // ts
    const = lambda shape: pl.BlockSpec(shape, lambda b, s: (0,) * len(shape))
    return pl.pallas_call(
        functools.partial(_mix_kernel, ts=ts),
        grid=(bsz, ns),
        in_specs=[pl.BlockSpec((1, ts, D_MODEL), lambda b, s: (b, s, 0)),
                  pl.BlockSpec((1, SUBLANES, D_MODEL), lambda b, s: (b, 0, 0)),
                  const(gmix.shape), const(gffn.shape), const(w_main.shape), const(w_dt.shape),
                  const(cwa.shape), const(cws.shape), const(cbs.shape), const(dtb.shape),
                  const(aneg.shape), const(dskip.shape), const(gssm.shape),
                  const(wout_a.shape), const(wout_b.shape), const(wr_t.shape)],
        out_specs=[pl.BlockSpec((1, ts, D_MODEL), lambda b, s: (b, s, 0)),
                   pl.BlockSpec((1, ts, D_MODEL), lambda b, s: (b, s, 0)),
                   pl.BlockSpec((N_EXPERTS, ts), lambda b, s: (0, b * ns + s))],
        out_shape=[jax.ShapeDtypeStruct((bsz, seq, D_MODEL), F32),
                   jax.ShapeDtypeStruct((bsz, seq, D_MODEL), BF16),
                   jax.ShapeDtypeStruct((N_EXPERTS, bsz * seq), F32)],
        scratch_shapes=[pltpu.VMEM((SUBLANES, D_CONV), F32),
                        pltpu.VMEM((SUBLANES, D_XBC), F32),
                        pltpu.VMEM((SSM_GROUPS, SSM_STATE, GROUP_W), F32),
                        pltpu.VMEM((ts, D_XBC), F32),
                        pltpu.VMEM((ts, LANES), F32),
                        pltpu.VMEM((ts, D_SSM), F32)],
        compiler_params=pltpu.CompilerParams(dimension_semantics=("arbitrary", "arbitrary"),
                                             vmem_limit_bytes=VMEM_LIMIT),
    )(x, modrows, gmix, gffn, w_main, w_dt, cwa, cws, cbs, dtb, aneg, dskip, gssm,
      wout_a, wout_b, wr_t)


def _route_kernel(lg_ref, bias_ref, gate_ref, cnt_ref):
    scores = _sigmoid(lg_ref[...])
    sel = scores + bias_ref[...]
    n = sel.shape[1]
    neg = -jnp.inf
    idx8 = lax.broadcasted_iota(jnp.int32, (EXPERTS_PER_GROUP, n), 0)
    grp_scores = []
    for g in range(N_EXPERT_GROUPS):
        sg = sel[g * EXPERTS_PER_GROUP:(g + 1) * EXPERTS_PER_GROUP]
        m1 = jnp.max(sg, axis=0, keepdims=True)
        first = jnp.min(jnp.where(sg == m1, idx8, EXPERTS_PER_GROUP), axis=0, keepdims=True)
        m2 = jnp.max(jnp.where(idx8 == first, neg, sg), axis=0, keepdims=True)
        grp_scores.append(m1 + m2)
    masked = []
    for g in range(N_EXPERT_GROUPS):
        rank = jnp.zeros((1, n), jnp.int32)
        for o in range(N_EXPERT_GROUPS):
            if o == g:
                continue
            ahead = (grp_scores[o] >= grp_scores[g]) if o < g else (grp_scores[o] > grp_scores[g])
            rank = rank + ahead.astype(jnp.int32)
        sg = sel[g * EXPERTS_PER_GROUP:(g + 1) * EXPERTS_PER_GROUP]
        masked.append(jnp.where(rank < TOPK_GROUPS, sg, neg))
    vals = jnp.concatenate(masked, axis=0)
    idx = lax.broadcasted_iota(jnp.int32, (N_EXPERTS, n), 0)
    picked = jnp.zeros((N_EXPERTS, n), F32)
    for _ in range(TOP_K):
        m = jnp.max(vals, axis=0, keepdims=True)
        first = jnp.min(jnp.where(vals == m, idx, N_EXPERTS), axis=0, keepdims=True)
        hit = idx == first
        picked = jnp.where(hit, scores, picked)
        vals = jnp.where(hit, neg, vals)
    gates = picked / jnp.sum(picked, axis=0, keepdims=True) * ROUTED_SCALE
    gate_ref[...] = gates
    cnt_ref[0] = jnp.sum(jnp.where(gates > 0.0, 1.0, 0.0), axis=1, keepdims=True)


def _route_call(logits_t, bias_col, ts):
    n_tok = logits_t.shape[1]
    nt = n_tok // ts
    return pl.pallas_call(
        _route_kernel,
        grid=(nt,),
        in_specs=[pl.BlockSpec((N_EXPERTS, ts), lambda i: (0, i)),
                  pl.BlockSpec((N_EXPERTS, 1), lambda i: (0, 0))],
        out_specs=[pl.BlockSpec((N_EXPERTS, ts), lambda i: (0, i)),
                   pl.BlockSpec((1, N_EXPERTS, 1), lambda i: (i, 0, 0))],
        out_shape=[jax.ShapeDtypeStruct((N_EXPERTS, n_tok), F32),
                   jax.ShapeDtypeStruct((nt, N_EXPERTS, 1), F32)],
        compiler_params=pltpu.CompilerParams(dimension_semantics=("arbitrary",),
                                             vmem_limit_bytes=VMEM_LIMIT),
    )(logits_t, bias_col)


RUN_ALIGN = 16
ROW_BLOCK = 512
EXPERT_BLOCK = 1024
RUN_UNROLL = 8


def _run_capacity(ts):
    worst = ts * TOP_K + N_EXPERTS * (RUN_ALIGN - 1)
    return -(-worst // ROW_BLOCK) * ROW_BLOCK


def _for_each_run(cpad_sm, dst_sm, tile, fn):
    def body(e, off):
        n = pl.multiple_of(cpad_sm[tile * N_EXPERTS + e], RUN_ALIGN)
        d = pl.multiple_of(dst_sm[tile * N_EXPERTS + e], RUN_ALIGN)
        o = pl.multiple_of(off, RUN_ALIGN)

        @pl.when(n > 0)
        def _():
            fn(o, d, n)
        return off + n
    lax.fori_loop(0, N_EXPERTS, body, 0, unroll=RUN_UNROLL)


def _dispatch_kernel(cpad_sm, dst_sm, rtot_sm, tail_off_sm, tail_n_sm,
                     h2_ref, gt_ref, loff_ref, cnt_ref, xs_hbm, stage, zbuf, sems, zsem, *, ts):
    t = pl.program_id(0)
    nt = pl.num_programs(0)
    slot = t % 2

    def run_copy(slot_):
        return lambda o, d, n: pltpu.make_async_copy(
            stage.at[slot_, pl.ds(o, n)], xs_hbm.at[pl.ds(d, n)], sems.at[slot_])

    def tail_copies(fn):
        def body(e, c):
            n = pl.multiple_of(tail_n_sm[e], RUN_ALIGN)
            d = pl.multiple_of(tail_off_sm[e], RUN_ALIGN)

            @pl.when(n > 0)
            def _():
                fn(pltpu.make_async_copy(zbuf.at[pl.ds(0, n)], xs_hbm.at[pl.ds(d, n)], zsem))
            return c
        lax.fori_loop(0, N_EXPERTS, body, 0)

    @pl.when(t == 0)
    def _():
        zbuf[...] = jnp.zeros_like(zbuf)
        tail_copies(lambda c: c.start())

    @pl.when(t >= 2)
    def _():
        _for_each_run(cpad_sm, dst_sm, t - 2, lambda o, d, n: run_copy(slot)(o, d, n).wait())

    gt = gt_ref[...]
    chosen = gt > 0.0
    ii = lax.broadcasted_iota(jnp.int32, (ts, ts), 0)
    jj = lax.broadcasted_iota(jnp.int32, (ts, ts), 1)
    before = jnp.where(ii < jj, 1.0, 0.0).astype(BF16)
    prefix = _dot(jnp.where(chosen, 1.0, 0.0).astype(BF16), before)
    sp = jnp.where(chosen, prefix + 1.0, 0.0).astype(BF16)
    loff = loff_ref[0]
    cnt = cnt_ref[0]
    h2 = h2_ref[...]

    def block(k, c):
        r0 = pl.multiple_of(k * ROW_BLOCK, ROW_BLOCK)
        r = (lax.broadcasted_iota(jnp.int32, (ROW_BLOCK, N_EXPERTS), 0) + r0).astype(F32)
        own = jnp.where(r >= loff, jnp.where(r < loff + cnt, 1.0, 0.0), 0.0)
        want = r[:, :1] + 1.0 - jnp.sum(own * loff, axis=1, keepdims=True)
        v = _dot(own.astype(BF16), sp)
        onehot = jnp.where(v == want, 1.0, 0.0).astype(BF16)
        stage[slot, pl.ds(r0, ROW_BLOCK), :] = _dot(onehot, h2).astype(BF16)
        return c

    lax.fori_loop(0, (rtot_sm[t] + ROW_BLOCK - 1) // ROW_BLOCK, block, 0)
    _for_each_run(cpad_sm, dst_sm, t, lambda o, d, n: run_copy(slot)(o, d, n).start())

    @pl.when(t == nt - 1)
    def _():
        _for_each_run(cpad_sm, dst_sm, t, lambda o, d, n: run_copy(slot)(o, d, n).wait())

        @pl.when(t >= 1)
        def _():
            _for_each_run(cpad_sm, dst_sm, t - 1,
                          lambda o, d, n: run_copy(1 - slot)(o, d, n).wait())
        tail_copies(lambda c: c.wait())


def _dispatch_call(meta, h2, gates_t, ts, n_rows):
    n_tok = h2.shape[0]
    nt = n_tok // ts
    rcap = _run_capacity(ts)
    grid_spec = pltpu.PrefetchScalarGridSpec(
        num_scalar_prefetch=5, grid=(nt,),
        in_specs=[pl.BlockSpec((ts, D_MODEL), lambda t, *_: (t, 0)),
                  pl.BlockSpec((N_EXPERTS, ts), lambda t, *_: (0, t)),
                  pl.BlockSpec((1, 1, N_EXPERTS), lambda t, *_: (t, 0, 0)),
                  pl.BlockSpec((1, 1, N_EXPERTS), lambda t, *_: (t, 0, 0))],
        out_specs=pl.BlockSpec(memory_space=pl.ANY),
        scratch_shapes=[pltpu.VMEM((2, rcap, D_MODEL), BF16),
                        pltpu.VMEM((EXPERT_BLOCK, D_MODEL), BF16),
                        pltpu.SemaphoreType.DMA((2,)),
                        pltpu.SemaphoreType.DMA(())])
    return pl.pallas_call(
        functools.partial(_dispatch_kernel, ts=ts),
        grid_spec=grid_spec,
        out_shape=jax.ShapeDtypeStruct((n_rows, D_MODEL), BF16),
        compiler_params=pltpu.CompilerParams(dimension_semantics=("arbitrary",),
                                             vmem_limit_bytes=VMEM_LIMIT),
    )(meta["cpad"], meta["dst"], meta["rtot"], meta["tail_off"], meta["tail_n"],
      h2, gates_t, meta["loff_row"], meta["cnt_row"])


def _expert_kernel(bexp_sm, nvalid_sm, x_ref, wgu_ref, wd_ref, y_ref):
    @pl.when(pl.program_id(0) < nvalid_sm[0])
    def _():
        gu = _dot(x_ref[...], wgu_ref[0])
        hid = _silu(gu[:, :D_EXPERT]) * gu[:, D_EXPERT:]
        y_ref[...] = _dot(hid.astype(BF16), wd_ref[0]).astype(BF16)


def _expert_call(meta, xs, wgu, wd):
    n_rows = xs.shape[0]
    last = lambda b, nv: jnp.minimum(b, nv[0] - 1)
    grid_spec = pltpu.PrefetchScalarGridSpec(
        num_scalar_prefetch=2, grid=(n_rows // EXPERT_BLOCK,),
        in_specs=[pl.BlockSpec((EXPERT_BLOCK, D_MODEL), lambda b, be, nv: (last(b, nv), 0)),
                  pl.BlockSpec((1, D_MODEL, 2 * D_EXPERT), lambda b, be, nv: (be[last(b, nv)], 0, 0)),
                  pl.BlockSpec((1, D_EXPERT, D_MODEL), lambda b, be, nv: (be[last(b, nv)], 0, 0))],
        out_specs=pl.BlockSpec((EXPERT_BLOCK, D_MODEL), lambda b, be, nv: (last(b, nv), 0)))
    return pl.pallas_call(
        _expert_kernel,
        grid_spec=grid_spec,
        out_shape=jax.ShapeDtypeStruct((n_rows, D_MODEL), BF16),
        compiler_params=pltpu.CompilerParams(dimension_semantics=("arbitrary",),
                                             vmem_limit_bytes=VMEM_LIMIT),
    )(meta["bexp"], meta["nvalid"], xs, wgu, wd)


def _combine_kernel(cpad_sm, dst_sm, rtot_sm,
                    gk_ref, loff_ref, cnt_ref, h2_ref, x1_ref, mod_ref, sgu_ref, sd_ref, gfin_ref,
                    ys_hbm, o_ref, stage, sems, acc_ref, *, ts):
    t = pl.program_id(0)
    nt = pl.num_programs(0)
    slot = t % 2

    def run_copy(slot_):
        return lambda o, d, n: pltpu.make_async_copy(
            ys_hbm.at[pl.ds(d, n)], stage.at[slot_, pl.ds(o, n)], sems.at[slot_])

    @pl.when(t == 0)
    def _():
        stage[...] = jnp.zeros_like(stage)
        _for_each_run(cpad_sm, dst_sm, t, lambda o, d, n: run_copy(slot)(o, d, n).start())

    @pl.when(t + 1 < nt)
    def _():
        _for_each_run(cpad_sm, dst_sm, t + 1, lambda o, d, n: run_copy(1 - slot)(o, d, n).start())

    gk = gk_ref[...]
    chosen = gk > 0.0
    ii = lax.broadcasted_iota(jnp.int32, (ts, ts), 0)
    jj = lax.broadcasted_iota(jnp.int32, (ts, ts), 1)
    earlier = jnp.where(jj < ii, 1.0, 0.0).astype(BF16)
    prefix = _dot(earlier, jnp.where(chosen, 1.0, 0.0).astype(BF16))
    sp = jnp.where(chosen, prefix + 1.0, 0.0).astype(BF16)
    spg = jnp.concatenate([sp, gk.astype(BF16)], axis=0)
    loff = loff_ref[0]
    cnt = cnt_ref[0]
    h2 = h2_ref[...]
    sgu = _dot(h2, sgu_ref[...])
    acc_ref[...] = _dot((_silu(sgu[:, :D_EXPERT]) * sgu[:, D_EXPERT:]).astype(BF16), sd_ref[...])

    _for_each_run(cpad_sm, dst_sm, t, lambda o, d, n: run_copy(slot)(o, d, n).wait())

    def block(k, c):
        r0 = pl.multiple_of(k * ROW_BLOCK, ROW_BLOCK)
        r = (lax.broadcasted_iota(jnp.int32, (N_EXPERTS, ROW_BLOCK), 1) + r0).astype(F32)
        own = jnp.where(r >= loff, jnp.where(r < loff + cnt, 1.0, 0.0), 0.0)
        want = r[:1] + 1.0 - jnp.sum(own * loff, axis=0, keepdims=True)
        ownb = own.astype(BF16)
        vg = _dot(spg, ownb)
        w = jnp.where(vg[:ts] == want, vg[ts:], 0.0).astype(BF16)
        acc_ref[...] += _dot(w, stage[slot, pl.ds(r0, ROW_BLOCK), :])
        return c

    lax.fori_loop(0, (rtot_sm[t] + ROW_BLOCK - 1) // ROW_BLOCK, block, 0)
    x2 = x1_ref[...] + mod_ref[0][5:6] * acc_ref[...]
    o_ref[...] = _rms(x2) * gfin_ref[...]


def _combine_call(meta, gates_k, h2, x1, modrows, sgu, sd, gfin, ys, ts, seq):
    n_tok = h2.shape[0]
    nt = n_tok // ts
    rcap = _run_capacity(ts)
    tiles_per_batch = seq // ts
    const = lambda shape: pl.BlockSpec(shape, lambda t, *_: (0,) * len(shape))
    grid_spec = pltpu.PrefetchScalarGridSpec(
        num_scalar_prefetch=3, grid=(nt,),
        in_specs=[pl.BlockSpec((ts, N_EXPERTS), lambda t, *_: (t, 0)),
                  pl.BlockSpec((1, N_EXPERTS, 1), lambda t, *_: (t, 0, 0)),
                  pl.BlockSpec((1, N_EXPERTS, 1), lambda t, *_: (t, 0, 0)),
                  pl.BlockSpec((ts, D_MODEL), lambda t, *_: (t, 0)),
                  pl.BlockSpec((ts, D_MODEL), lambda t, *_: (t, 0)),
                  pl.BlockSpec((1, SUBLANES, D_MODEL), lambda t, *_: (t // tiles_per_batch, 0, 0)),
                  const(sgu.shape), const(sd.shape), const(gfin.shape),
                  pl.BlockSpec(memory_space=pl.ANY)],
        out_specs=pl.BlockSpec((ts, D_MODEL), lambda t, *_: (t, 0)),
        scratch_shapes=[pltpu.VMEM((2, rcap, D_MODEL), BF16),
                        pltpu.SemaphoreType.DMA((2,)),
                        pltpu.VMEM((ts, D_MODEL), F32)])
    return pl.pallas_call(
        functools.partial(_combine_kernel, ts=ts),
        grid_spec=grid_spec,
        out_shape=jax.ShapeDtypeStruct((n_tok, D_MODEL), F32),
        compiler_params=pltpu.CompilerParams(dimension_semantics=("arbitrary",),
                                             vmem_limit_bytes=VMEM_LIMIT),
    )(meta["cpad"], meta["dst"], meta["rtot"], gates_k, meta["loff_col"], meta["cnt_col"],
      h2, x1, modrows, sgu, sd, gfin, ys)


def _dispatch_plan(cnt, ts, n_tok):
    nt = cnt.shape[0]
    cnt = cnt.astype(jnp.int32)
    cpad = (cnt + RUN_ALIGN - 1) // RUN_ALIGN * RUN_ALIGN
    loff = jnp.cumsum(cpad, axis=1) - cpad
    ctot = jnp.sum(cpad, axis=0)
    cblk = (ctot + EXPERT_BLOCK - 1) // EXPERT_BLOCK * EXPERT_BLOCK
    ends = jnp.cumsum(cblk)
    base = ends - cblk
    dst = base[None, :] + jnp.cumsum(cpad, axis=0) - cpad
    n_rows = n_tok * TOP_K + nt * N_EXPERTS * (RUN_ALIGN - 1) + N_EXPERTS * EXPERT_BLOCK
    n_rows = -(-n_rows // EXPERT_BLOCK) * EXPERT_BLOCK
    starts = jnp.arange(n_rows // EXPERT_BLOCK, dtype=jnp.int32) * EXPERT_BLOCK
    bexp = jnp.minimum(jnp.sum(starts[:, None] >= ends[None, :], axis=1), N_EXPERTS - 1)
    meta = dict(cpad=cpad.reshape(-1), dst=dst.reshape(-1).astype(jnp.int32),
                rtot=jnp.sum(cpad, axis=1), tail_off=(base + ctot).astype(jnp.int32),
                tail_n=(cblk - ctot).astype(jnp.int32), bexp=bexp.astype(jnp.int32),
                nvalid=(ends[-1:] // EXPERT_BLOCK).astype(jnp.int32),
                loff_row=loff.astype(F32).reshape(nt, 1, N_EXPERTS),
                cnt_row=cnt.astype(F32).reshape(nt, 1, N_EXPERTS),
                loff_col=loff.astype(F32).reshape(nt, N_EXPERTS, 1),
                cnt_col=cnt.astype(F32).reshape(nt, N_EXPERTS, 1))
    return meta, n_rows


def _tile(n, pref):
    return pref if n % pref == 0 else n


def kernel(x, c, w_ada, b_ada, norm_mix_g, w_in, conv_a_w, ssm_conv_w, ssm_conv_b, ssm_dt_bias,
           ssm_a_log, ssm_d, ssm_norm_g, w_out, norm_ffn_g, w_router, router_bias, we_gate, we_up,
           we_down, ws_gate, ws_up, ws_down, norm_final_g):
    bsz, seq, _ = x.shape
    n_tok = bsz * seq
    assert w_ada.shape[0] == 1, "single-layer block"
    l = 0
    ts = _tile(seq, 256)
    n_main = 4 * D_CONV + D_XBC
    pad_heads = lambda a: jnp.pad(a.reshape(1, -1), ((0, 0), (0, LANES - SSM_HEADS)))
    mod = _ada_call(c, w_ada[l], b_ada[l])
    modrows = jnp.pad(jnp.transpose(mod, (1, 0, 2)), ((0, 0), (0, SUBLANES - 6), (0, 0)))
    w_main = w_in[l][:, :n_main].astype(BF16)
    w_dt = jnp.pad(w_in[l][:, n_main:], ((0, 0), (0, LANES - SSM_HEADS))).astype(BF16)
    wr_t = jnp.transpose(w_router[l])
    wr_hi = wr_t.astype(BF16)
    wr_split = jnp.stack([wr_hi, (wr_t - wr_hi.astype(F32)).astype(BF16)])
    x1, h2, logits_t = _mix_call(
        x, modrows, norm_mix_g[l].reshape(1, -1), norm_ffn_g[l].reshape(1, -1), w_main, w_dt,
        conv_a_w[l], ssm_conv_w[l], ssm_conv_b[l].reshape(1, -1), pad_heads(ssm_dt_bias[l]),
        pad_heads(-jnp.exp(ssm_a_log[l])), jnp.repeat(ssm_d[l], SSM_HEADDIM).reshape(1, -1),
        ssm_norm_g[l].reshape(1, -1), w_out[l][:D_CONV].astype(BF16),
        w_out[l][D_CONV:].astype(BF16), wr_split, _tile(seq, 512))
    h2 = h2.reshape(n_tok, D_MODEL)
    gates_t, cnt = _route_call(logits_t, router_bias[l].reshape(-1, 1), ts)
    meta, n_rows = _dispatch_plan(cnt[:, :, 0], ts, n_tok)
    xs = _dispatch_call(meta, h2, gates_t, ts, n_rows)
    wgu = jnp.concatenate([we_gate[l], we_up[l]], axis=-1).astype(BF16)
    ys = _expert_call(meta, xs, wgu, we_down[l].astype(BF16))
    sgu = jnp.concatenate([ws_gate[l], ws_up[l]], axis=-1).astype(BF16)
    out = _combine_call(meta, jnp.transpose(gates_t), h2, x1.reshape(n_tok, D_MODEL), modrows,
                        sgu, ws_down[l].astype(BF16), norm_final_g.reshape(1, -1), ys, ts, seq)
    return out.reshape(bsz, seq, D_MODEL)
```

```python
import functools

import jax
import jax.numpy as jnp
from jax import lax
from jax.experimental import pallas as pl
from jax.experimental.pallas import tpu as pltpu

D_MODEL = 1024
D_CONV = 512
D_SSM = 512
SSM_HEADS = 8
SSM_HEADDIM = 64
SSM_GROUPS = 2
SSM_STATE = 128
HEADS_PER_GROUP = SSM_HEADS // SSM_GROUPS
GROUP_W = HEADS_PER_GROUP * SSM_HEADDIM
D_XBC = D_SSM + 2 * SSM_GROUPS * SSM_STATE
CHUNK = 64
N_EXPERTS = 64
N_EXPERT_GROUPS = 8
EXPERTS_PER_GROUP = N_EXPERTS // N_EXPERT_GROUPS
TOPK_GROUPS = 4
TOP_K = 8
D_EXPERT = 256
ROUTED_SCALE = 2.5
EPS = 1e-6
LANES = 128
SUBLANES = 8
VMEM_LIMIT = 56 * 1024 * 1024

F32 = jnp.float32
BF16 = jnp.bfloat16
HI = lax.Precision.HIGHEST


def _dot(a, b, precision=None):
    return jnp.dot(a, b, preferred_element_type=F32, precision=precision)


def _dot_nt(a, b, precision=None):
    return lax.dot_general(a, b, (((1,), (1,)), ((), ())), preferred_element_type=F32,
                           precision=precision)


def _dot_tn(a, b):
    return lax.dot_general(a, b, (((0,), (0,)), ((), ())), preferred_element_type=F32)


def _sigmoid(v):
    return 1.0 / (1.0 + jnp.exp(-v))


def _silu(v):
    return v * _sigmoid(v)


def _rms(v):
    return v * lax.rsqrt(jnp.mean(v * v, axis=-1, keepdims=True) + EPS)


def _split3(v):
    a = v.astype(BF16)
    r = v - a.astype(F32)
    b = r.astype(BF16)
    return a, b, (r - b.astype(F32)).astype(BF16)


def _ada_kernel(c_ref, w_ref, b_ref, o_ref):
    o_ref[0] = _dot(c_ref[...], w_ref[...], HI) + b_ref[...]


def _ada_call(c, w_ada, b_ada):
    bsz = c.shape[0]
    return pl.pallas_call(
        _ada_kernel,
        grid=(6,),
        in_specs=[pl.BlockSpec((bsz, D_MODEL), lambda j: (0, 0)),
                  pl.BlockSpec((D_MODEL, D_MODEL), lambda j: (0, j)),
                  pl.BlockSpec((1, D_MODEL), lambda j: (0, j))],
        out_specs=pl.BlockSpec((1, bsz, D_MODEL), lambda j: (j, 0, 0)),
        out_shape=jax.ShapeDtypeStruct((6, bsz, D_MODEL), F32),
        compiler_params=pltpu.CompilerParams(dimension_semantics=("arbitrary",),
                                             vmem_limit_bytes=VMEM_LIMIT),
    )(c, w_ada, b_ada.reshape(1, -1))


def _shift_rows(v, halo, k):
    rolled = pltpu.roll(v, k, 0)
    hrolled = pltpu.roll(halo, k, 0)
    rid = lax.broadcasted_iota(jnp.int32, (SUBLANES, v.shape[1]), 0)
    top = jnp.where(rid < k, hrolled, rolled[:SUBLANES])
    return jnp.concatenate([top, rolled[SUBLANES:]], axis=0)


def _mix_kernel(x_ref, mod_ref, gmix_ref, gffn_ref, win_ref, wdt_ref, cwa_ref, cws_ref, cbs_ref,
                dtb_ref, aneg_ref, dskip_ref, gssm_ref, wouta_ref, woutb_ref, wrt_ref,
                x1_ref, h2_ref, lg_ref,
                halo_v, halo_x, st_ref, xbc_s, dt_s, y_s, *, ts):
    s_idx = pl.program_id(1)

    @pl.when(s_idx == 0)
    def _():
        halo_v[...] = jnp.zeros_like(halo_v)
        halo_x[...] = jnp.zeros_like(halo_x)
        st_ref[...] = jnp.zeros_like(st_ref)

    x = x_ref[0]
    mod = mod_ref[0]
    h = _rms(x) * (gmix_ref[...] * (1.0 + mod[1:2])) + mod[0:1]
    hb = h.astype(BF16)

    def proj(lo, hi):
        return _dot(hb, win_ref[:, lo:hi])

    v = proj(D_CONV, 2 * D_CONV) * proj(2 * D_CONV, 3 * D_CONV)
    hv = halo_v[...]
    cwa = cwa_ref[...]
    conv_a = (cwa[2:3] * v + cwa[1:2] * _shift_rows(v, hv, 1) + cwa[0:1] * _shift_rows(v, hv, 2))
    halo_v[...] = v[ts - SUBLANES:]
    ya = (proj(0, D_CONV) * conv_a).astype(BF16)

    u = proj(4 * D_CONV, 4 * D_CONV + D_XBC)
    hx = halo_x[...]
    cws = cws_ref[...]
    conv_s = (cws[3:4] * u + cws[2:3] * _shift_rows(u, hx, 1) + cws[1:2] * _shift_rows(u, hx, 2)
              + cws[0:1] * _shift_rows(u, hx, 3)) + cbs_ref[...]
    halo_x[...] = u[ts - SUBLANES:]
    xbc_s[...] = _silu(conv_s)
    dt_raw = _dot(hb, wdt_ref[...]) + dtb_ref[...]
    dt_s[...] = jnp.maximum(dt_raw, 0.0) + jnp.log1p(jnp.exp(-jnp.abs(dt_raw)))

    li = lax.broadcasted_iota(jnp.int32, (CHUNK, D_SSM), 0)
    sj = lax.broadcasted_iota(jnp.int32, (CHUNK, D_SSM), 1) & (SSM_HEADDIM - 1)
    causal = sj <= li
    upto = (li <= sj).astype(F32)
    eh = lax.broadcasted_iota(jnp.int32, (LANES, D_SSM), 0)
    ej = lax.broadcasted_iota(jnp.int32, (LANES, D_SSM), 1) // SSM_HEADDIM
    expand = jnp.where(eh == ej, 1.0, 0.0).astype(BF16)
    expand3 = jnp.concatenate([expand] * 3, axis=0)
    tl = lax.broadcasted_iota(jnp.int32, (CHUNK, 3 * CHUNK), 0)
    tc = lax.broadcasted_iota(jnp.int32, (CHUNK, 3 * CHUNK), 1) & (CHUNK - 1)
    tril3 = jnp.where(tc <= tl, 1.0, 0.0).astype(BF16)
    bi = lax.broadcasted_iota(jnp.int32, (GROUP_W, GROUP_W), 0) // SSM_HEADDIM
    bj = lax.broadcasted_iota(jnp.int32, (GROUP_W, GROUP_W), 1) // SSM_HEADDIM
    blockmask = (bi == bj).astype(F32)
    aneg = aneg_ref[...]
    dskip = dskip_ref[...]

    def chunk_body(c, carry):
        r0 = pl.multiple_of(c * CHUNK, CHUNK)
        xs = xbc_s[pl.ds(r0, CHUNK), 0:D_SSM]
        bm = xbc_s[pl.ds(r0, CHUNK), D_SSM:D_SSM + SSM_GROUPS * SSM_STATE]
        cm = xbc_s[pl.ds(r0, CHUNK), D_SSM + SSM_GROUPS * SSM_STATE:D_XBC]
        dt = dt_s[pl.ds(r0, CHUNK), :]
        both = _dot(jnp.concatenate(_split3(jnp.concatenate([dt * aneg, dt], axis=0)), axis=1),
                    expand3)
        a_exp = both[:CHUNK]
        dt_exp = both[CHUNK:]
        cs = _dot(tril3, jnp.concatenate(_split3(a_exp), axis=0))
        cs_row = jnp.sum(a_exp * upto, axis=0, keepdims=True)
        lmat = jnp.where(causal, jnp.exp(cs - cs_row), 0.0)
        dtx = xs * dt_exp
        out_decay = jnp.exp(cs)
        cs_last = cs[CHUNK - 1:CHUNK]
        end_decay = jnp.exp(cs_last - cs)
        chunk_decay = jnp.exp(cs_last)
        ys = []
        for g in range(SSM_GROUPS):
            lo, hi = g * GROUP_W, (g + 1) * GROUP_W
            bg = bm[:, g * SSM_STATE:(g + 1) * SSM_STATE].astype(BF16)
            cg = cm[:, g * SSM_STATE:(g + 1) * SSM_STATE].astype(BF16)
            brep = jnp.concatenate([bg] * HEADS_PER_GROUP, axis=0)
            w = (_dot_nt(cg, brep) * lmat[:, lo:hi]).astype(BF16)
            dtx_g = dtx[:, lo:hi]
            bd = (jnp.concatenate([dtx_g] * HEADS_PER_GROUP, axis=0) * blockmask).astype(BF16)
            st = st_ref[g]
            y_g = _dot(w, bd) + _dot(cg, st.astype(BF16)) * out_decay[:, lo:hi]
            new = _dot_tn(bg, (dtx_g * end_decay[:, lo:hi]).astype(BF16))
            st_ref[g] = st * chunk_decay[:, lo:hi] + new
            ys.append(y_g)
        y_s[pl.ds(r0, CHUNK), :] = jnp.concatenate(ys, axis=1) + dskip * xs
        return carry

    lax.fori_loop(0, ts // CHUNK, chunk_body, 0, unroll=True)

    z = proj(3 * D_CONV, 4 * D_CONV)
    yb = (_rms(y_s[...] * _silu(z)) * gssm_ref[...]).astype(BF16)

    out = _dot(ya, wouta_ref[...]) + _dot(yb, woutb_ref[...])
    x1 = x + mod[2:3] * out
    x1_ref[0] = x1
    h2 = _rms(x1) * (gffn_ref[...] * (1.0 + mod[4:5])) + mod[3:4]
    h2_hi = h2.astype(BF16)
    h2_ref[0] = h2_hi
    h2_lo = (h2 - h2_hi.astype(F32)).astype(BF16)
    w_hi = wrt_ref[0]
    lg_ref[...] = _dot_nt(w_hi, h2_hi) + (_dot_nt(w_hi, h2_lo) + _dot_nt(wrt_ref[1], h2_hi))


def _mix_call(x, modrows, gmix, gffn, w_main, w_dt, cwa, cws, cbs, dtb, aneg, dskip, gssm,
              wout_a, wout_b, wr_t, ts):
    bsz, seq, _ = x.shape
    ns = seq // ts
    const = lambda shape: pl.BlockSpec(shape, lambda b, s: (0,) * len(shape))
    return pl.pallas_call(
        functools.partial(_mix_kernel, ts=ts),
        grid=(bsz, ns),
        in_specs=[pl.BlockSpec((1, ts, D_MODEL), lambda b, s: (b, s, 0)),
                  pl.BlockSpec((1, SUBLANES, D_MODEL), lambda b, s: (b, 0, 0)),
                  const(gmix.shape), const(gffn.shape), const(w_main.shape), const(w_dt.shape),
                  const(cwa.shape), const(cws.shape), const(cbs.shape), const(dtb.shape),
                  const(aneg.shape), const(dskip.shape), const(gssm.shape),
                  const(wout_a.shape), const(wout_b.shape), const(wr_t.shape)],
        out_specs=[pl.BlockSpec((1, ts, D_MODEL), lambda b, s: (b, s, 0)),
                   pl.BlockSpec((1, ts, D_MODEL), lambda b, s: (b, s, 0)),
                   pl.BlockSpec((N_EXPERTS, ts), lambda b, s: (0, b * ns + s))],
        out_shape=[jax.ShapeDtypeStruct((bsz, seq, D_MODEL), F32),
                   jax.ShapeDtypeStruct((bsz, seq, D_MODEL), BF16),
                   jax.ShapeDtypeStruct((N_EXPERTS, bsz * seq), F32)],
        scratch_shapes=[pltpu.VMEM((SUBLANES, D_CONV), F32),
                        pltpu.VMEM((SUBLANES, D_XBC), F32),
                        pltpu.VMEM((SSM_GROUPS, SSM_STATE, GROUP_W), F32),
                        pltpu.VMEM((ts, D_XBC), F32),
                        pltpu.VMEM((ts, LANES), F32),
                        pltpu.VMEM((ts, D_SSM), F32)],
        compiler_params=pltpu.CompilerParams(dimension_semantics=("arbitrary", "arbitrary"),
                                             vmem_limit_bytes=VMEM_LIMIT),
    )(x, modrows, gmix, gffn, w_main, w_dt, cwa, cws, cbs, dtb, aneg, dskip, gssm,
      wout_a, wout_b, wr_t)


def _route_kernel(lg_ref, bias_ref, gate_ref, cnt_ref):
    scores = _sigmoid(lg_ref[...])
    sel = scores + bias_ref[...]
    n = sel.shape[1]
    neg = -jnp.inf
    idx8 = lax.broadcasted_iota(jnp.int32, (EXPERTS_PER_GROUP, n), 0)
    grp_scores = []
    for g in range(N_EXPERT_GROUPS):
        sg = sel[g * EXPERTS_PER_GROUP:(g + 1) * EXPERTS_PER_GROUP]
        m1 = jnp.max(sg, axis=0, keepdims=True)
        first = jnp.min(jnp.where(sg == m1, idx8, EXPERTS_PER_GROUP), axis=0, keepdims=True)
        m2 = jnp.max(jnp.where(idx8 == first, neg, sg), axis=0, keepdims=True)
        grp_scores.append(m1 + m2)
    masked = []
    for g in range(N_EXPERT_GROUPS):
        rank = jnp.zeros((1, n), jnp.int32)
        for o in range(N_EXPERT_GROUPS):
            if o == g:
                continue
            ahead = (grp_scores[o] >= grp_scores[g]) if o < g else (grp_scores[o] > grp_scores[g])
            rank = rank + ahead.astype(jnp.int32)
        sg = sel[g * EXPERTS_PER_GROUP:(g + 1) * EXPERTS_PER_GROUP]
        masked.append(jnp.where(rank < TOPK_GROUPS, sg, neg))
    vals = jnp.concatenate(masked, axis=0)
    idx = lax.broadcasted_iota(jnp.int32, (N_EXPERTS, n), 0)
    picked = jnp.zeros((N_EXPERTS, n), F32)
    for _ in range(TOP_K):
        m = jnp.max(vals, axis=0, keepdims=True)
        first = jnp.min(jnp.where(vals == m, idx, N_EXPERTS), axis=0, keepdims=True)
        hit = idx == first
        picked = jnp.where(hit, scores, picked)
        vals = jnp.where(hit, neg, vals)
    gates = picked / jnp.sum(picked, axis=0, keepdims=True) * ROUTED_SCALE
    gate_ref[...] = gates
    cnt_ref[0] = jnp.sum(jnp.where(gates > 0.0, 1.0, 0.0), axis=1, keepdims=True)


def _route_call(logits_t, bias_col, ts):
    n_tok = logits_t.shape[1]
    nt = n_tok // ts
    return pl.pallas_call(
        _route_kernel,
        grid=(nt,),
        in_specs=[pl.BlockSpec((N_EXPERTS, ts), lambda i: (0, i)),
                  pl.BlockSpec((N_EXPERTS, 1), lambda i: (0, 0))],
        out_specs=[pl.BlockSpec((N_EXPERTS, ts), lambda i: (0, i)),
                   pl.BlockSpec((1, N_EXPERTS, 1), lambda i: (i, 0, 0))],
        out_shape=[jax.ShapeDtypeStruct((N_EXPERTS, n_tok), F32),
                   jax.ShapeDtypeStruct((nt, N_EXPERTS, 1), F32)],
        compiler_params=pltpu.CompilerParams(dimension_semantics=("arbitrary",),
                                             vmem_limit_bytes=VMEM_LIMIT),
    )(logits_t, bias_col)


RUN_ALIGN = 16
ROW_BLOCK = 1024
EXPERT_BLOCK = 1024
RUN_UNROLL = 8


def _run_capacity(ts):
    worst = ts * TOP_K + N_EXPERTS * (RUN_ALIGN - 1)
    return -(-worst // ROW_BLOCK) * ROW_BLOCK


def _for_each_run(cpad_sm, dst_sm, tile, fn):
    def body(e, off):
        n = pl.multiple_of(cpad_sm[tile * N_EXPERTS + e], RUN_ALIGN)
        d = pl.multiple_of(dst_sm[tile * N_EXPERTS + e], RUN_ALIGN)
        o = pl.multiple_of(off, RUN_ALIGN)

        @pl.when(n > 0)
        def _():
            fn(o, d, n)
        return off + n
    lax.fori_loop(0, N_EXPERTS, body, 0, unroll=RUN_UNROLL)


def _dispatch_kernel(cpad_sm, dst_sm, rtot_sm, tail_off_sm, tail_n_sm,
                     h2_ref, gt_ref, loff_ref, cnt_ref, xs_hbm, stage, zbuf, sems, zsem, *, ts):
    t = pl.program_id(0)
    nt = pl.num_programs(0)
    slot = t % 2

    def run_copy(slot_):
        return lambda o, d, n: pltpu.make_async_copy(
            stage.at[slot_, pl.ds(o, n)], xs_hbm.at[pl.ds(d, n)], sems.at[slot_])

    def tail_copies(fn):
        def body(e, c):
            n = pl.multiple_of(tail_n_sm[e], RUN_ALIGN)
            d = pl.multiple_of(tail_off_sm[e], RUN_ALIGN)

            @pl.when(n > 0)
            def _():
                fn(pltpu.make_async_copy(zbuf.at[pl.ds(0, n)], xs_hbm.at[pl.ds(d, n)], zsem))
            return c
        lax.fori_loop(0, N_EXPERTS, body, 0)

    @pl.when(t == 0)
    def _():
        zbuf[...] = jnp.zeros_like(zbuf)
        tail_copies(lambda c: c.start())

    @pl.when(t >= 2)
    def _():
        _for_each_run(cpad_sm, dst_sm, t - 2, lambda o, d, n: run_copy(slot)(o, d, n).wait())

    gt = gt_ref[...]
    chosen = gt > 0.0
    ii = lax.broadcasted_iota(jnp.int32, (ts, ts), 0)
    jj = lax.broadcasted_iota(jnp.int32, (ts, ts), 1)
    before = jnp.where(ii < jj, 1.0, 0.0).astype(BF16)
    prefix = _dot(jnp.where(chosen, 1.0, 0.0).astype(BF16), before)
    sp = jnp.where(chosen, prefix + 1.0, 0.0).astype(BF16)
    loff = loff_ref[0]
    cnt = cnt_ref[0]
    h2 = h2_ref[...]

    def block(k, c):
        r0 = pl.multiple_of(k * ROW_BLOCK, ROW_BLOCK)
        r = (lax.broadcasted_iota(jnp.int32, (ROW_BLOCK, N_EXPERTS), 0) + r0).astype(F32)
        own = jnp.where(r >= loff, jnp.where(r < loff + cnt, 1.0, 0.0), 0.0)
        want = r[:, :1] + 1.0 - jnp.sum(own * loff, axis=1, keepdims=True)
        v = _dot(own.astype(BF16), sp)
        onehot = jnp.where(v == want, 1.0, 0.0).astype(BF16)
        stage[slot, pl.ds(r0, ROW_BLOCK), :] = _dot(onehot, h2).astype(BF16)
        return c

    lax.fori_loop(0, (rtot_sm[t] + ROW_BLOCK - 1) // ROW_BLOCK, block, 0)
    _for_each_run(cpad_sm, dst_sm, t, lambda o, d, n: run_copy(slot)(o, d, n).start())

    @pl.when(t == nt - 1)
    def _():
        _for_each_run(cpad_sm, dst_sm, t, lambda o, d, n: run_copy(slot)(o, d, n).wait())

        @pl.when(t >= 1)
        def _():
            _for_each_run(cpad_sm, dst_sm, t - 1,
                          lambda o, d, n: run_copy(1 - slot)(o, d, n).wait())
        tail_copies(lambda c: c.wait())


def _dispatch_call(meta, h2, gates_t, ts, n_rows):
    n_tok = h2.shape[0]
    nt = n_tok // ts
    rcap = _run_capacity(ts)
    grid_spec = pltpu.PrefetchScalarGridSpec(
        num_scalar_prefetch=5, grid=(nt,),
        in_specs=[pl.BlockSpec((ts, D_MODEL), lambda t, *_: (t, 0)),
                  pl.BlockSpec((N_EXPERTS, ts), lambda t, *_: (0, t)),
                  pl.BlockSpec((1, 1, N_EXPERTS), lambda t, *_: (t, 0, 0)),
                  pl.BlockSpec((1, 1, N_EXPERTS), lambda t, *_: (t, 0, 0))],
        out_specs=pl.BlockSpec(memory_space=pl.ANY),
        scratch_shapes=[pltpu.VMEM((2, rcap, D_MODEL), BF16),
                        pltpu.VMEM((EXPERT_BLOCK, D_MODEL), BF16),
                        pltpu.SemaphoreType.DMA((2,)),
                        pltpu.SemaphoreType.DMA(())])
    return pl.pallas_call(
        functools.partial(_dispatch_kernel, ts=ts),
        grid_spec=grid_spec,
        out_shape=jax.ShapeDtypeStruct((n_rows, D_MODEL), BF16),
        compiler_params=pltpu.CompilerParams(dimension_semantics=("arbitrary",),
                                             vmem_limit_bytes=VMEM_LIMIT),
    )(meta["cpad"], meta["dst"], meta["rtot"], meta["tail_off"], meta["tail_n"],
      h2, gates_t, meta["loff_row"], meta["cnt_row"])


def _expert_kernel(bexp_sm, nvalid_sm, x_ref, wg_ref, wu_ref, wd_ref, y_ref):
    @pl.when(pl.program_id(0) < nvalid_sm[0])
    def _():
        x = x_ref[...]
        hid = _silu(_dot(x, wg_ref[0].astype(BF16))) * _dot(x, wu_ref[0].astype(BF16))
        y_ref[...] = _dot(hid.astype(BF16), wd_ref[0].astype(BF16)).astype(BF16)


def _expert_call(meta, xs, wg, wu, wd):
    n_rows = xs.shape[0]
    last = lambda b, nv: jnp.minimum(b, nv[0] - 1)
    grid_spec = pltpu.PrefetchScalarGridSpec(
        num_scalar_prefetch=2, grid=(n_rows // EXPERT_BLOCK,),
        in_specs=[pl.BlockSpec((EXPERT_BLOCK, D_MODEL), lambda b, be, nv: (last(b, nv), 0)),
                  pl.BlockSpec((1, D_MODEL, D_EXPERT), lambda b, be, nv: (be[last(b, nv)], 0, 0)),
                  pl.BlockSpec((1, D_MODEL, D_EXPERT), lambda b, be, nv: (be[last(b, nv)], 0, 0)),
                  pl.BlockSpec((1, D_EXPERT, D_MODEL), lambda b, be, nv: (be[last(b, nv)], 0, 0))],
        out_specs=pl.BlockSpec((EXPERT_BLOCK, D_MODEL), lambda b, be, nv: (last(b, nv), 0)))
    return pl.pallas_call(
        _expert_kernel,
        grid_spec=grid_spec,
        out_shape=jax.ShapeDtypeStruct((n_rows, D_MODEL), BF16),
        compiler_params=pltpu.CompilerParams(dimension_semantics=("arbitrary",),
                                             vmem_limit_bytes=VMEM_LIMIT),
    )(meta["bexp"], meta["nvalid"], xs, wg, wu, wd)


def _combine_kernel(cpad_sm, dst_sm, rtot_sm,
                    gk_ref, loff_ref, cnt_ref, h2_ref, x1_ref, mod_ref, sgu_ref, sd_ref, gfin_ref,
                    ys_hbm, o_ref, stage, sems, acc_ref, *, ts):
    t = pl.program_id(0)
    nt = pl.num_programs(0)
    slot = t % 2

    def run_copy(slot_):
        return lambda o, d, n: pltpu.make_async_copy(
            ys_hbm.at[pl.ds(d, n)], stage.at[slot_, pl.ds(o, n)], sems.at[slot_])

    @pl.when(t == 0)
    def _():
        stage[...] = jnp.zeros_like(stage)
        _for_each_run(cpad_sm, dst_sm, t, lambda o, d, n: run_copy(slot)(o, d, n).start())

    @pl.when(t + 1 < nt)
    def _():
        _for_each_run(cpad_sm, dst_sm, t + 1, lambda o, d, n: run_copy(1 - slot)(o, d, n).start())

    gk = gk_ref[...]
    chosen = gk > 0.0
    ii = lax.broadcasted_iota(jnp.int32, (ts, ts), 0)
    jj = lax.broadcasted_iota(jnp.int32, (ts, ts), 1)
    earlier = jnp.where(jj < ii, 1.0, 0.0).astype(BF16)
    prefix = _dot(earlier, jnp.where(chosen, 1.0, 0.0).astype(BF16))
    sp = jnp.where(chosen, prefix + 1.0, 0.0).astype(BF16)
    spg = jnp.concatenate([sp, gk.astype(BF16)], axis=0)
    loff = loff_ref[0]
    cnt = cnt_ref[0]
    h2 = h2_ref[...]
    sgu = _dot(h2, sgu_ref[...])
    acc_ref[...] = _dot((_silu(sgu[:, :D_EXPERT]) * sgu[:, D_EXPERT:]).astype(BF16), sd_ref[...])

    _for_each_run(cpad_sm, dst_sm, t, lambda o, d, n: run_copy(slot)(o, d, n).wait())

    def block(k, c):
        r0 = pl.multiple_of(k * ROW_BLOCK, ROW_BLOCK)
        r = (lax.broadcasted_iota(jnp.int32, (N_EXPERTS, ROW_BLOCK), 1) + r0).astype(F32)
        own = jnp.where(r >= loff, jnp.where(r < loff + cnt, 1.0, 0.0), 0.0)
        want = r[:1] + 1.0 - jnp.sum(own * loff, axis=0, keepdims=True)
        ownb = own.astype(BF16)
        vg = _dot(spg, ownb)
        w = jnp.where(vg[:ts] == want, vg[ts:], 0.0).astype(BF16)
        acc_ref[...] += _dot(w, stage[slot, pl.ds(r0, ROW_BLOCK), :])
        return c

    lax.fori_loop(0, (rtot_sm[t] + ROW_BLOCK - 1) // ROW_BLOCK, block, 0)
    x2 = x1_ref[...] + mod_ref[0][5:6] * acc_ref[...]
    o_ref[...] = _rms(x2) * gfin_ref[...]


def _combine_call(meta, gates_k, h2, x1, modrows, sgu, sd, gfin, ys, ts, seq):
    n_tok = h2.shape[0]
    nt = n_tok // ts
    rcap = _run_capacity(ts)
    tiles_per_batch = seq // ts
    const = lambda shape: pl.BlockSpec(shape, lambda t, *_: (0,) * len(shape))
    grid_spec = pltpu.PrefetchScalarGridSpec(
        num_scalar_prefetch=3, grid=(nt,),
        in_specs=[pl.BlockSpec((ts, N_EXPERTS), lambda t, *_: (t, 0)),
                  pl.BlockSpec((1, N_EXPERTS, 1), lambda t, *_: (t, 0, 0)),
                  pl.BlockSpec((1, N_EXPERTS, 1), lambda t, *_: (t, 0, 0)),
                  pl.BlockSpec((ts, D_MODEL), lambda t, *_: (t, 0)),
                  pl.BlockSpec((ts, D_MODEL), lambda t, *_: (t, 0)),
                  pl.BlockSpec((1, SUBLANES, D_MODEL), lambda t, *_: (t // tiles_per_batch, 0, 0)),
                  const(sgu.shape), const(sd.shape), const(gfin.shape),
                  pl.BlockSpec(memory_space=pl.ANY)],
        out_specs=pl.BlockSpec((ts, D_MODEL), lambda t, *_: (t, 0)),
        scratch_shapes=[pltpu.VMEM((2, rcap, D_MODEL), BF16),
                        pltpu.SemaphoreType.DMA((2,)),
                        pltpu.VMEM((ts, D_MODEL), F32)])
    return pl.pallas_call(
        functools.partial(_combine_kernel, ts=ts),
        grid_spec=grid_spec,
        out_shape=jax.ShapeDtypeStruct((n_tok, D_MODEL), F32),
        compiler_params=pltpu.CompilerParams(dimension_semantics=("arbitrary",),
                                             vmem_limit_bytes=VMEM_LIMIT),
    )(meta["cpad"], meta["dst"], meta["rtot"], gates_k, meta["loff_col"], meta["cnt_col"],
      h2, x1, modrows, sgu, sd, gfin, ys)


def _dispatch_plan(cnt, ts, n_tok):
    nt = cnt.shape[0]
    cnt = cnt.astype(jnp.int32)
    cpad = (cnt + RUN_ALIGN - 1) // RUN_ALIGN * RUN_ALIGN
    loff = jnp.cumsum(cpad, axis=1) - cpad
    ctot = jnp.sum(cpad, axis=0)
    cblk = (ctot + EXPERT_BLOCK - 1) // EXPERT_BLOCK * EXPERT_BLOCK
    ends = jnp.cumsum(cblk)
    base = ends - cblk
    dst = base[None, :] + jnp.cumsum(cpad, axis=0) - cpad
    n_rows = n_tok * TOP_K + nt * N_EXPERTS * (RUN_ALIGN - 1) + N_EXPERTS * EXPERT_BLOCK
    n_rows = -(-n_rows // EXPERT_BLOCK) * EXPERT_BLOCK
    starts = jnp.arange(n_rows // EXPERT_BLOCK, dtype=jnp.int32) * EXPERT_BLOCK
    bexp = jnp.minimum(jnp.sum(starts[:, None] >= ends[None, :], axis=1), N_EXPERTS - 1)
    meta = dict(cpad=cpad.reshape(-1), dst=dst.reshape(-1).astype(jnp.int32),
                rtot=jnp.sum(cpad, axis=1), tail_off=(base + ctot).astype(jnp.int32),
                tail_n=(cblk - ctot).astype(jnp.int32), bexp=bexp.astype(jnp.int32),
                nvalid=(ends[-1:] // EXPERT_BLOCK).astype(jnp.int32),
                loff_row=loff.astype(F32).reshape(nt, 1, N_EXPERTS),
                cnt_row=cnt.astype(F32).reshape(nt, 1, N_EXPERTS),
                loff_col=loff.astype(F32).reshape(nt, N_EXPERTS, 1),
                cnt_col=cnt.astype(F32).reshape(nt, N_EXPERTS, 1))
    return meta, n_rows


def _tile(n, pref):
    return pref if n % pref == 0 else n


def kernel(x, c, w_ada, b_ada, norm_mix_g, w_in, conv_a_w, ssm_conv_w, ssm_conv_b, ssm_dt_bias,
           ssm_a_log, ssm_d, ssm_norm_g, w_out, norm_ffn_g, w_router, router_bias, we_gate, we_up,
           we_down, ws_gate, ws_up, ws_down, norm_final_g):
    bsz, seq, _ = x.shape
    n_tok = bsz * seq
    assert w_ada.shape[0] == 1, "single-layer block"
    l = 0
    ts = _tile(seq, 256)
    n_main = 4 * D_CONV + D_XBC
    pad_heads = lambda a: jnp.pad(a.reshape(1, -1), ((0, 0), (0, LANES - SSM_HEADS)))
    mod = _ada_call(c, w_ada[l], b_ada[l])
    modrows = jnp.pad(jnp.transpose(mod, (1, 0, 2)), ((0, 0), (0, SUBLANES - 6), (0, 0)))
    w_main = w_in[l][:, :n_main].astype(BF16)
    w_dt = jnp.pad(w_in[l][:, n_main:], ((0, 0), (0, LANES - SSM_HEADS))).astype(BF16)
    wr_t = jnp.transpose(w_router[l])
    wr_hi = wr_t.astype(BF16)
    wr_split = jnp.stack([wr_hi, (wr_t - wr_hi.astype(F32)).astype(BF16)])
    x1, h2, logits_t = _mix_call(
        x, modrows, norm_mix_g[l].reshape(1, -1), norm_ffn_g[l].reshape(1, -1), w_main, w_dt,
        conv_a_w[l], ssm_conv_w[l], ssm_conv_b[l].reshape(1, -1), pad_heads(ssm_dt_bias[l]),
        pad_heads(-jnp.exp(ssm_a_log[l])), jnp.repeat(ssm_d[l], SSM_HEADDIM).reshape(1, -1),
        ssm_norm_g[l].reshape(1, -1), w_out[l][:D_CONV].astype(BF16),
        w_out[l][D_CONV:].astype(BF16), wr_split, _tile(seq, 512))
    h2 = h2.reshape(n_tok, D_MODEL)
    gates_t, cnt = _route_call(logits_t, router_bias[l].reshape(-1, 1), ts)
    meta, n_rows = _dispatch_plan(cnt[:, :, 0], ts, n_tok)
    xs = _dispatch_call(meta, h2, gates_t, ts, n_rows)
    ys = _expert_call(meta, xs, we_gate[l], we_up[l], we_down[l])
    sgu = jnp.concatenate([ws_gate[l], ws_up[l]], axis=-1).astype(BF16)
    out = _combine_call(meta, jnp.transpose(gates_t), h2, x1.reshape(n_tok, D_MODEL), modrows,
                        sgu, ws_down[l].astype(BF16), norm_final_g.reshape(1, -1), ys, ts, seq)
    return out.reshape(bsz, seq, D_MODEL)
```

```python
import functools

import jax
import jax.numpy as jnp
from jax import lax
from jax.experimental import pallas as pl
from jax.experimental.pallas import tpu as pltpu

D_MODEL = 1024
D_CONV = 512
D_SSM = 512
SSM_HEADS = 8
SSM_HEADDIM = 64
SSM_GROUPS = 2
SSM_STATE = 128
HEADS_PER_GROUP = SSM_HEADS // SSM_GROUPS
GROUP_W = HEADS_PER_GROUP * SSM_HEADDIM
D_XBC = D_SSM + 2 * SSM_GROUPS * SSM_STATE
CHUNK = 64
N_EXPERTS = 64
N_EXPERT_GROUPS = 8
EXPERTS_PER_GROUP = N_EXPERTS // N_EXPERT_GROUPS
TOPK_GROUPS = 4
TOP_K = 8
D_EXPERT = 256
ROUTED_SCALE = 2.5
EPS = 1e-6
LANES = 128
SUBLANES = 8
VMEM_LIMIT = 56 * 1024 * 1024

F32 = jnp.float32
BF16 = jnp.bfloat16
HI = lax.Precision.HIGHEST


def _dot(a, b, precision=None):
    return jnp.dot(a, b, preferred_element_type=F32, precision=precision)


def _dot_nt(a, b, precision=None):
    return lax.dot_general(a, b, (((1,), (1,)), ((), ())), preferred_element_type=F32,
                           precision=precision)


def _dot_tn(a, b):
    return lax.dot_general(a, b, (((0,), (0,)), ((), ())), preferred_element_type=F32)


def _sigmoid(v):
    return 1.0 / (1.0 + jnp.exp(-v))


def _silu(v):
    return v * _sigmoid(v)


def _rms(v):
    return v * lax.rsqrt(jnp.mean(v * v, axis=-1, keepdims=True) + EPS)


def _split3(v):
    a = v.astype(BF16)
    r = v - a.astype(F32)
    b = r.astype(BF16)
    return a, b, (r - b.astype(F32)).astype(BF16)


def _ada_kernel(c_ref, w_ref, b_ref, o_ref):
    o_ref[0] = _dot(c_ref[...], w_ref[...], HI) + b_ref[...]


def _ada_call(c, w_ada, b_ada):
    bsz = c.shape[0]
    return pl.pallas_call(
        _ada_kernel,
        grid=(6,),
        in_specs=[pl.BlockSpec((bsz, D_MODEL), lambda j: (0, 0)),
                  pl.BlockSpec((D_MODEL, D_MODEL), lambda j: (0, j)),
                  pl.BlockSpec((1, D_MODEL), lambda j: (0, j))],
        out_specs=pl.BlockSpec((1, bsz, D_MODEL), lambda j: (j, 0, 0)),
        out_shape=jax.ShapeDtypeStruct((6, bsz, D_MODEL), F32),
        compiler_params=pltpu.CompilerParams(dimension_semantics=("arbitrary",),
                                             vmem_limit_bytes=VMEM_LIMIT),
    )(c, w_ada, b_ada.reshape(1, -1))


def _shift_rows(v, halo, k):
    rolled = pltpu.roll(v, k, 0)
    hrolled = pltpu.roll(halo, k, 0)
    rid = lax.broadcasted_iota(jnp.int32, (SUBLANES, v.shape[1]), 0)
    top = jnp.where(rid < k, hrolled, rolled[:SUBLANES])
    return jnp.concatenate([top, rolled[SUBLANES:]], axis=0)


def _mix_kernel(x_ref, mod_ref, gmix_ref, gffn_ref, win_ref, wdt_ref, cwa_ref, cws_ref, cbs_ref,
                dtb_ref, aneg_ref, dskip_ref, gssm_ref, wouta_ref, woutb_ref, wrt_ref,
                x1_ref, h2_ref, lg_ref,
                halo_v, halo_x, st_ref, xbc_s, dt_s, y_s, *, ts):
    s_idx = pl.program_id(1)

    @pl.when(s_idx == 0)
    def _():
        halo_v[...] = jnp.zeros_like(halo_v)
        halo_x[...] = jnp.zeros_like(halo_x)
        st_ref[...] = jnp.zeros_like(st_ref)

    x = x_ref[0]
    mod = mod_ref[0]
    h = _rms(x) * (gmix_ref[...] * (1.0 + mod[1:2])) + mod[0:1]
    hb = h.astype(BF16)

    def proj(lo, hi):
        return _dot(hb, win_ref[:, lo:hi])

    v = proj(D_CONV, 2 * D_CONV) * proj(2 * D_CONV, 3 * D_CONV)
    hv = halo_v[...]
    cwa = cwa_ref[...]
    conv_a = (cwa[2:3] * v + cwa[1:2] * _shift_rows(v, hv, 1) + cwa[0:1] * _shift_rows(v, hv, 2))
    halo_v[...] = v[ts - SUBLANES:]
    ya = (proj(0, D_CONV) * conv_a).astype(BF16)

    u = proj(4 * D_CONV, 4 * D_CONV + D_XBC)
    hx = halo_x[...]
    cws = cws_ref[...]
    conv_s = (cws[3:4] * u + cws[2:3] * _shift_rows(u, hx, 1) + cws[1:2] * _shift_rows(u, hx, 2)
              + cws[0:1] * _shift_rows(u, hx, 3)) + cbs_ref[...]
    halo_x[...] = u[ts - SUBLANES:]
    xbc_s[...] = _silu(conv_s)
    dt_raw = _dot(hb, wdt_ref[...]) + dtb_ref[...]
    dt_s[...] = jnp.maximum(dt_raw, 0.0) + jnp.log1p(jnp.exp(-jnp.abs(dt_raw)))

    li = lax.broadcasted_iota(jnp.int32, (CHUNK, D_SSM), 0)
    sj = lax.broadcasted_iota(jnp.int32, (CHUNK, D_SSM), 1) & (SSM_HEADDIM - 1)
    causal = sj <= li
    upto = (li <= sj).astype(F32)
    eh = lax.broadcasted_iota(jnp.int32, (LANES, D_SSM), 0)
    ej = lax.broadcasted_iota(jnp.int32, (LANES, D_SSM), 1) // SSM_HEADDIM
    expand = jnp.where(eh == ej, 1.0, 0.0).astype(BF16)
    expand3 = jnp.concatenate([expand] * 3, axis=0)
    tl = lax.broadcasted_iota(jnp.int32, (CHUNK, 3 * CHUNK), 0)
    tc = lax.broadcasted_iota(jnp.int32, (CHUNK, 3 * CHUNK), 1) & (CHUNK - 1)
    tril3 = jnp.where(tc <= tl, 1.0, 0.0).astype(BF16)
    bi = lax.broadcasted_iota(jnp.int32, (GROUP_W, GROUP_W), 0) // SSM_HEADDIM
    bj = lax.broadcasted_iota(jnp.int32, (GROUP_W, GROUP_W), 1) // SSM_HEADDIM
    blockmask = (bi == bj).astype(F32)
    aneg = aneg_ref[...]
    dskip = dskip_ref[...]

    def chunk_body(c, carry):
        r0 = pl.multiple_of(c * CHUNK, CHUNK)
        xs = xbc_s[pl.ds(r0, CHUNK), 0:D_SSM]
        bm = xbc_s[pl.ds(r0, CHUNK), D_SSM:D_SSM + SSM_GROUPS * SSM_STATE]
        cm = xbc_s[pl.ds(r0, CHUNK), D_SSM + SSM_GROUPS * SSM_STATE:D_XBC]
        dt = dt_s[pl.ds(r0, CHUNK), :]
        both = _dot(jnp.concatenate(_split3(jnp.concatenate([dt * aneg, dt], axis=0)), axis=1),
                    expand3)
        a_exp = both[:CHUNK]
        dt_exp = both[CHUNK:]
        cs = _dot(tril3, jnp.concatenate(_split3(a_exp), axis=0))
        cs_row = jnp.sum(a_exp * upto, axis=0, keepdims=True)
        lmat = jnp.where(causal, jnp.exp(cs - cs_row), 0.0)
        dtx = xs * dt_exp
        out_decay = jnp.exp(cs)
        cs_last = cs[CHUNK - 1:CHUNK]
        end_decay = jnp.exp(cs_last - cs)
        chunk_decay = jnp.exp(cs_last)
        ys = []
        for g in range(SSM_GROUPS):
            lo, hi = g * GROUP_W, (g + 1) * GROUP_W
            bg = bm[:, g * SSM_STATE:(g + 1) * SSM_STATE].astype(BF16)
            cg = cm[:, g * SSM_STATE:(g + 1) * SSM_STATE].astype(BF16)
            brep = jnp.concatenate([bg] * HEADS_PER_GROUP, axis=0)
            w = (_dot_nt(cg, brep) * lmat[:, lo:hi]).astype(BF16)
            dtx_g = dtx[:, lo:hi]
            bd = (jnp.concatenate([dtx_g] * HEADS_PER_GROUP, axis=0) * blockmask).astype(BF16)
            st = st_ref[g]
            y_g = _dot(w, bd) + _dot(cg, st.astype(BF16)) * out_decay[:, lo:hi]
            new = _dot_tn(bg, (dtx_g * end_decay[:, lo:hi]).astype(BF16))
            st_ref[g] = st * chunk_decay[:, lo:hi] + new
            ys.append(y_g)
        y_s[pl.ds(r0, CHUNK), :] = jnp.concatenate(ys, axis=1) + dskip * xs
        return carry

    lax.fori_loop(0, ts // CHUNK, chunk_body, 0, unroll=True)

    z = proj(3 * D_CONV, 4 * D_CONV)
    yb = (_rms(y_s[...] * _silu(z)) * gssm_ref[...]).astype(BF16)

    out = _dot(ya, wouta_ref[...]) + _dot(yb, woutb_ref[...])
    x1 = x + mod[2:3] * out
    x1_ref[0] = x1
    h2 = _rms(x1) * (gffn_ref[...] * (1.0 + mod[4:5])) + mod[3:4]
    h2_hi = h2.astype(BF16)
    h2_ref[0] = h2_hi
    h2_lo = (h2 - h2_hi.astype(F32)).astype(BF16)
    w_hi = wrt_ref[0]
    lg_ref[...] = _dot_nt(w_hi, h2_hi) + (_dot_nt(w_hi, h2_lo) + _dot_nt(wrt_ref[1], h2_hi))


def _mix_call(x, modrows, gmix, gffn, w_main, w_dt, cwa, cws, cbs, dtb, aneg, dskip, gssm,
              wout_a, wout_b, wr_t, ts):
    bsz, seq, _ = x.shape
    ns = seq // ts
    const = lambda shape: pl.BlockSpec(shape, lambda b, s: (0,) * len(shape))
    return pl.pallas_call(
        functools.partial(_mix_kernel, ts=ts),
        grid=(bsz, ns),
        in_specs=[pl.BlockSpec((1, ts, D_MODEL), lambda b, s: (b, s, 0)),
                  pl.BlockSpec((1, SUBLANES, D_MODEL), lambda b, s: (b, 0, 0)),
                  const(gmix.shape), const(gffn.shape), const(w_main.shape), const(w_dt.shape),
                  const(cwa.shape), const(cws.shape), const(cbs.shape), const(dtb.shape),
                  const(aneg.shape), const(dskip.shape), const(gssm.shape),
                  const(wout_a.shape), const(wout_b.shape), const(wr_t.shape)],
        out_specs=[pl.BlockSpec((1, ts, D_MODEL), lambda b, s: (b, s, 0)),
                   pl.BlockSpec((1, ts, D_MODEL), lambda b, s: (b, s, 0)),
                   pl.BlockSpec((N_EXPERTS, ts), lambda b, s: (0, b * ns + s))],
        out_shape=[jax.ShapeDtypeStruct((bsz, seq, D_MODEL), F32),
                   jax.ShapeDtypeStruct((bsz, seq, D_MODEL), BF16),
                   jax.ShapeDtypeStruct((N_EXPERTS, bsz * seq), F32)],
        scratch_shapes=[pltpu.VMEM((SUBLANES, D_CONV), F32),
                        pltpu.VMEM((SUBLANES, D_XBC), F32),
                        pltpu.VMEM((SSM_GROUPS, SSM_STATE, GROUP_W), F32),
                        pltpu.VMEM((ts, D_XBC), F32),
                        pltpu.VMEM((ts, LANES), F32),
                        pltpu.VMEM((ts, D_SSM), F32)],
        compiler_params=pltpu.CompilerParams(dimension_semantics=("arbitrary", "arbitrary"),
                                             vmem_limit_bytes=VMEM_LIMIT),
    )(x, modrows, gmix, gffn, w_main, w_dt, cwa, cws, cbs, dtb, aneg, dskip, gssm,
      wout_a, wout_b, wr_t)


def _route_kernel(lg_ref, bias_ref, gate_ref, cnt_ref):
    scores = _sigmoid(lg_ref[...])
    sel = scores + bias_ref[...]
    n = sel.shape[1]
    neg = -jnp.inf
    idx8 = lax.broadcasted_iota(jnp.int32, (EXPERTS_PER_GROUP, n), 0)
    grp_scores = []
    for g in range(N_EXPERT_GROUPS):
        sg = sel[g * EXPERTS_PER_GROUP:(g + 1) * EXPERTS_PER_GROUP]
        m1 = jnp.max(sg, axis=0, keepdims=True)
        first = jnp.min(jnp.where(sg == m1, idx8, EXPERTS_PER_GROUP), axis=0, keepdims=True)
        m2 = jnp.max(jnp.where(idx8 == first, neg, sg), axis=0, keepdims=True)
        grp_scores.append(m1 + m2)
    masked = []
    for g in range(N_EXPERT_GROUPS):
        rank = jnp.zeros((1, n), jnp.int32)
        for o in range(N_EXPERT_GROUPS):
            if o == g:
                continue
            ahead = (grp_scores[o] >= grp_scores[g]) if o < g else (grp_scores[o] > grp_scores[g])
            rank = rank + ahead.astype(jnp.int32)
        sg = sel[g * EXPERTS_PER_GROUP:(g + 1) * EXPERTS_PER_GROUP]
        masked.append(jnp.where(rank < TOPK_GROUPS, sg, neg))
    vals = jnp.concatenate(masked, axis=0)
    idx = lax.broadcasted_iota(jnp.int32, (N_EXPERTS, n), 0)
    picked = jnp.zeros((N_EXPERTS, n), F32)
    for _ in range(TOP_K):
        m = jnp.max(vals, axis=0, keepdims=True)
        first = jnp.min(jnp.where(vals == m, idx, N_EXPERTS), axis=0, keepdims=True)
        hit = idx == first
        picked = jnp.where(hit, scores, picked)
        vals = jnp.where(hit, neg, vals)
    gates = picked / jnp.sum(picked, axis=0, keepdims=True) * ROUTED_SCALE
    gate_ref[...] = gates
    cnt_ref[0] = jnp.sum(jnp.where(gates > 0.0, 1.0, 0.0), axis=1, keepdims=True)


def _route_call(logits_t, bias_col, ts):
    n_tok = logits_t.shape[1]
    nt = n_tok // ts
    return pl.pallas_call(
        _route_kernel,
        grid=(nt,),
        in_specs=[pl.BlockSpec((N_EXPERTS, ts), lambda i: (0, i)),
                  pl.BlockSpec((N_EXPERTS, 1), lambda i: (0, 0))],
        out_specs=[pl.BlockSpec((N_EXPERTS, ts), lambda i: (0, i)),
                   pl.BlockSpec((1, N_EXPERTS, 1), lambda i: (i, 0, 0))],
        out_shape=[jax.ShapeDtypeStruct((N_EXPERTS, n_tok), F32),
                   jax.ShapeDtypeStruct((nt, N_EXPERTS, 1), F32)],
        compiler_params=pltpu.CompilerParams(dimension_semantics=("arbitrary",),
                                             vmem_limit_bytes=VMEM_LIMIT),
    )(logits_t, bias_col)


RUN_ALIGN = 16
ROW_BLOCK = 1024
EXPERT_BLOCK = 1024
RUN_UNROLL = 8


def _run_capacity(ts):
    worst = ts * TOP_K + N_EXPERTS * (RUN_ALIGN - 1)
    return -(-worst // ROW_BLOCK) * ROW_BLOCK


def _for_each_run(cpad_sm, dst_sm, tile, fn):
    def body(e, off):
        n = pl.multiple_of(cpad_sm[tile * N_EXPERTS + e], RUN_ALIGN)
        d = pl.multiple_of(dst_sm[tile * N_EXPERTS + e], RUN_ALIGN)
        o = pl.multiple_of(off, RUN_ALIGN)

        @pl.when(n > 0)
        def _():
            fn(o, d, n)
        return off + n
    lax.fori_loop(0, N_EXPERTS, body, 0, unroll=RUN_UNROLL)


def _dispatch_kernel(cpad_sm, dst_sm, rtot_sm, tail_off_sm, tail_n_sm,
                     h2_ref, gt_ref, loff_ref, cnt_ref, xs_hbm, stage, zbuf, sems, zsem, *, ts):
    t = pl.program_id(0)
    nt = pl.num_programs(0)
    slot = t % 2

    def run_copy(slot_):
        return lambda o, d, n: pltpu.make_async_copy(
            stage.at[slot_, pl.ds(o, n)], xs_hbm.at[pl.ds(d, n)], sems.at[slot_])

    def wait_tile(tile, slot_):
        n = pl.multiple_of(rtot_sm[tile], RUN_ALIGN)

        @pl.when(n > 0)
        def _():
            run_copy(slot_)(0, 0, n).wait()

    def tail_copies(fn):
        def body(e, c):
            n = pl.multiple_of(tail_n_sm[e], RUN_ALIGN)
            d = pl.multiple_of(tail_off_sm[e], RUN_ALIGN)

            @pl.when(n > 0)
            def _():
                fn(pltpu.make_async_copy(zbuf.at[pl.ds(0, n)], xs_hbm.at[pl.ds(d, n)], zsem))
            return c
        lax.fori_loop(0, N_EXPERTS, body, 0)

    @pl.when(t == 0)
    def _():
        zbuf[...] = jnp.zeros_like(zbuf)
        tail_copies(lambda c: c.start())

    @pl.when(t >= 2)
    def _():
        wait_tile(t - 2, slot)

    gt = gt_ref[...]
    chosen = gt > 0.0
    ii = lax.broadcasted_iota(jnp.int32, (ts, ts), 0)
    jj = lax.broadcasted_iota(jnp.int32, (ts, ts), 1)
    before = jnp.where(ii < jj, 1.0, 0.0).astype(BF16)
    prefix = _dot(jnp.where(chosen, 1.0, 0.0).astype(BF16), before)
    sp = jnp.where(chosen, prefix + 1.0, 0.0).astype(BF16)
    loff = loff_ref[0]
    cnt = cnt_ref[0]
    h2 = h2_ref[...]

    def block(k, c):
        r0 = pl.multiple_of(k * ROW_BLOCK, ROW_BLOCK)
        r = (lax.broadcasted_iota(jnp.int32, (ROW_BLOCK, N_EXPERTS), 0) + r0).astype(F32)
        own = jnp.where(r >= loff, jnp.where(r < loff + cnt, 1.0, 0.0), 0.0)
        want = r[:, :1] + 1.0 - jnp.sum(own * loff, axis=1, keepdims=True)
        v = _dot(own.astype(BF16), sp)
        onehot = jnp.where(v == want, 1.0, 0.0).astype(BF16)
        stage[slot, pl.ds(r0, ROW_BLOCK), :] = _dot(onehot, h2).astype(BF16)
        return c

    lax.fori_loop(0, (rtot_sm[t] + ROW_BLOCK - 1) // ROW_BLOCK, block, 0)
    _for_each_run(cpad_sm, dst_sm, t, lambda o, d, n: run_copy(slot)(o, d, n).start())

    @pl.when(t == nt - 1)
    def _():
        wait_tile(t, slot)

        @pl.when(t >= 1)
        def _():
            wait_tile(t - 1, 1 - slot)
        tail_copies(lambda c: c.wait())


def _dispatch_call(meta, h2, gates_t, ts, n_rows):
    n_tok = h2.shape[0]
    nt = n_tok // ts
    rcap = _run_capacity(ts)
    grid_spec = pltpu.PrefetchScalarGridSpec(
        num_scalar_prefetch=5, grid=(nt,),
        in_specs=[pl.BlockSpec((ts, D_MODEL), lambda t, *_: (t, 0)),
                  pl.BlockSpec((N_EXPERTS, ts), lambda t, *_: (0, t)),
                  pl.BlockSpec((1, 1, N_EXPERTS), lambda t, *_: (t, 0, 0)),
                  pl.BlockSpec((1, 1, N_EXPERTS), lambda t, *_: (t, 0, 0))],
        out_specs=pl.BlockSpec(memory_space=pl.ANY),
        scratch_shapes=[pltpu.VMEM((2, rcap, D_MODEL), BF16),
                        pltpu.VMEM((EXPERT_BLOCK, D_MODEL), BF16),
                        pltpu.SemaphoreType.DMA((2,)),
                        pltpu.SemaphoreType.DMA(())])
    return pl.pallas_call(
        functools.partial(_dispatch_kernel, ts=ts),
        grid_spec=grid_spec,
        out_shape=jax.ShapeDtypeStruct((n_rows, D_MODEL), BF16),
        compiler_params=pltpu.CompilerParams(dimension_semantics=("arbitrary",),
                                             vmem_limit_bytes=VMEM_LIMIT),
    )(meta["cpad"], meta["dst"], meta["rtot"], meta["tail_off"], meta["tail_n"],
      h2, gates_t, meta["loff_row"], meta["cnt_row"])


def _expert_kernel(bexp_sm, nvalid_sm, x_ref, wg_ref, wu_ref, wd_ref, y_ref):
    @pl.when(pl.program_id(0) < nvalid_sm[0])
    def _():
        x = x_ref[...]
        hid = _silu(_dot(x, wg_ref[0].astype(BF16))) * _dot(x, wu_ref[0].astype(BF16))
        y_ref[...] = _dot(hid.astype(BF16), wd_ref[0].astype(BF16)).astype(BF16)


def _expert_call(meta, xs, wg, wu, wd):
    n_rows = xs.shape[0]
    last = lambda b, nv: jnp.minimum(b, nv[0] - 1)
    grid_spec = pltpu.PrefetchScalarGridSpec(
        num_scalar_prefetch=2, grid=(n_rows // EXPERT_BLOCK,),
        in_specs=[pl.BlockSpec((EXPERT_BLOCK, D_MODEL), lambda b, be, nv: (last(b, nv), 0)),
                  pl.BlockSpec((1, D_MODEL, D_EXPERT), lambda b, be, nv: (be[last(b, nv)], 0, 0)),
                  pl.BlockSpec((1, D_MODEL, D_EXPERT), lambda b, be, nv: (be[last(b, nv)], 0, 0)),
                  pl.BlockSpec((1, D_EXPERT, D_MODEL), lambda b, be, nv: (be[last(b, nv)], 0, 0))],
        out_specs=pl.BlockSpec((EXPERT_BLOCK, D_MODEL), lambda b, be, nv: (last(b, nv), 0)))
    return pl.pallas_call(
        _expert_kernel,
        grid_spec=grid_spec,
        out_shape=jax.ShapeDtypeStruct((n_rows, D_MODEL), BF16),
        compiler_params=pltpu.CompilerParams(dimension_semantics=("arbitrary",),
                                             vmem_limit_bytes=VMEM_LIMIT),
    )(meta["bexp"], meta["nvalid"], xs, wg, wu, wd)


def _combine_kernel(cpad_sm, dst_sm, rtot_sm,
                    gk_ref, loff_ref, cnt_ref, h2_ref, x1_ref, mod_ref, sgu_ref, sd_ref, gfin_ref,
                    ys_hbm, o_ref, stage, sems, acc_ref, *, ts):
    t = pl.program_id(0)
    nt = pl.num_programs(0)
    slot = t % 2

    def run_copy(slot_):
        return lambda o, d, n: pltpu.make_async_copy(
            ys_hbm.at[pl.ds(d, n)], stage.at[slot_, pl.ds(o, n)], sems.at[slot_])

    @pl.when(t == 0)
    def _():
        stage[...] = jnp.zeros_like(stage)
        _for_each_run(cpad_sm, dst_sm, t, lambda o, d, n: run_copy(slot)(o, d, n).start())

    @pl.when(t + 1 < nt)
    def _():
        _for_each_run(cpad_sm, dst_sm, t + 1, lambda o, d, n: run_copy(1 - slot)(o, d, n).start())

    gk = gk_ref[...]
    chosen = gk > 0.0
    ii = lax.broadcasted_iota(jnp.int32, (ts, ts), 0)
    jj = lax.broadcasted_iota(jnp.int32, (ts, ts), 1)
    earlier = jnp.where(jj < ii, 1.0, 0.0).astype(BF16)
    prefix = _dot(earlier, jnp.where(chosen, 1.0, 0.0).astype(BF16))
    sp = jnp.where(chosen, prefix + 1.0, 0.0).astype(BF16)
    spg = jnp.concatenate([sp, gk.astype(BF16)], axis=0)
    loff = loff_ref[0]
    cnt = cnt_ref[0]
    h2 = h2_ref[...]
    sgu = _dot(h2, sgu_ref[...])
    acc_ref[...] = _dot((_silu(sgu[:, :D_EXPERT]) * sgu[:, D_EXPERT:]).astype(BF16), sd_ref[...])

    n_tile = pl.multiple_of(rtot_sm[t], RUN_ALIGN)

    @pl.when(n_tile > 0)
    def _():
        run_copy(slot)(0, 0, n_tile).wait()

    def block(k, c):
        r0 = pl.multiple_of(k * ROW_BLOCK, ROW_BLOCK)
        r = (lax.broadcasted_iota(jnp.int32, (N_EXPERTS, ROW_BLOCK), 1) + r0).astype(F32)
        own = jnp.where(r >= loff, jnp.where(r < loff + cnt, 1.0, 0.0), 0.0)
        want = r[:1] + 1.0 - jnp.sum(own * loff, axis=0, keepdims=True)
        ownb = own.astype(BF16)
        vg = _dot(spg, ownb)
        w = jnp.where(vg[:ts] == want, vg[ts:], 0.0).astype(BF16)
        acc_ref[...] += _dot(w, stage[slot, pl.ds(r0, ROW_BLOCK), :])
        return c

    lax.fori_loop(0, (rtot_sm[t] + ROW_BLOCK - 1) // ROW_BLOCK, block, 0)
    x2 = x1_ref[...] + mod_ref[0][5:6] * acc_ref[...]
    o_ref[...] = _rms(x2) * gfin_ref[...]


def _combine_call(meta, gates_k, h2, x1, modrows, sgu, sd, gfin, ys, ts, seq):
    n_tok = h2.shape[0]
    nt = n_tok // ts
    rcap = _run_capacity(ts)
    tiles_per_batch = seq // ts
    const = lambda shape: pl.BlockSpec(shape, lambda t, *_: (0,) * len(shape))
    grid_spec = pltpu.PrefetchScalarGridSpec(
        num_scalar_prefetch=3, grid=(nt,),
        in_specs=[pl.BlockSpec((ts, N_EXPERTS), lambda t, *_: (t, 0)),
                  pl.BlockSpec((1, N_EXPERTS, 1), lambda t, *_: (t, 0, 0)),
                  pl.BlockSpec((1, N_EXPERTS, 1), lambda t, *_: (t, 0, 0)),
                  pl.BlockSpec((ts, D_MODEL), lambda t, *_: (t, 0)),
                  pl.BlockSpec((ts, D_MODEL), lambda t, *_: (t, 0)),
                  pl.BlockSpec((1, SUBLANES, D_MODEL), lambda t, *_: (t // tiles_per_batch, 0, 0)),
                  const(sgu.shape), const(sd.shape), const(gfin.shape),
                  pl.BlockSpec(memory_space=pl.ANY)],
        out_specs=pl.BlockSpec((ts, D_MODEL), lambda t, *_: (t, 0)),
        scratch_shapes=[pltpu.VMEM((2, rcap, D_MODEL), BF16),
                        pltpu.SemaphoreType.DMA((2,)),
                        pltpu.VMEM((ts, D_MODEL), F32)])
    return pl.pallas_call(
        functools.partial(_combine_kernel, ts=ts),
        grid_spec=grid_spec,
        out_shape=jax.ShapeDtypeStruct((n_tok, D_MODEL), F32),
        compiler_params=pltpu.CompilerParams(dimension_semantics=("arbitrary",),
                                             vmem_limit_bytes=VMEM_LIMIT),
    )(meta["cpad"], meta["dst"], meta["rtot"], gates_k, meta["loff_col"], meta["cnt_col"],
      h2, x1, modrows, sgu, sd, gfin, ys)


def _dispatch_plan(cnt, ts, n_tok):
    nt = cnt.shape[0]
    cnt = cnt.astype(jnp.int32)
    cpad = (cnt + RUN_ALIGN - 1) // RUN_ALIGN * RUN_ALIGN
    loff = jnp.cumsum(cpad, axis=1) - cpad
    ctot = jnp.sum(cpad, axis=0)
    cblk = (ctot + EXPERT_BLOCK - 1) // EXPERT_BLOCK * EXPERT_BLOCK
    ends = jnp.cumsum(cblk)
    base = ends - cblk
    dst = base[None, :] + jnp.cumsum(cpad, axis=0) - cpad
    n_rows = n_tok * TOP_K + nt * N_EXPERTS * (RUN_ALIGN - 1) + N_EXPERTS * EXPERT_BLOCK
    n_rows = -(-n_rows // EXPERT_BLOCK) * EXPERT_BLOCK
    starts = jnp.arange(n_rows // EXPERT_BLOCK, dtype=jnp.int32) * EXPERT_BLOCK
    bexp = jnp.minimum(jnp.sum(starts[:, None] >= ends[None, :], axis=1), N_EXPERTS - 1)
    meta = dict(cpad=cpad.reshape(-1), dst=dst.reshape(-1).astype(jnp.int32),
                rtot=jnp.sum(cpad, axis=1), tail_off=(base + ctot).astype(jnp.int32),
                tail_n=(cblk - ctot).astype(jnp.int32), bexp=bexp.astype(jnp.int32),
                nvalid=(ends[-1:] // EXPERT_BLOCK).astype(jnp.int32),
                loff_row=loff.astype(F32).reshape(nt, 1, N_EXPERTS),
                cnt_row=cnt.astype(F32).reshape(nt, 1, N_EXPERTS),
                loff_col=loff.astype(F32).reshape(nt, N_EXPERTS, 1),
                cnt_col=cnt.astype(F32).reshape(nt, N_EXPERTS, 1))
    return meta, n_rows


def _tile(n, pref):
    return pref if n % pref == 0 else n


def kernel(x, c, w_ada, b_ada, norm_mix_g, w_in, conv_a_w, ssm_conv_w, ssm_conv_b, ssm_dt_bias,
           ssm_a_log, ssm_d, ssm_norm_g, w_out, norm_ffn_g, w_router, router_bias, we_gate, we_up,
           we_down, ws_gate, ws_up, ws_down, norm_final_g):
    bsz, seq, _ = x.shape
    n_tok = bsz * seq
    assert w_ada.shape[0] == 1, "single-layer block"
    l = 0
    ts = _tile(seq, 256)
    n_main = 4 * D_CONV + D_XBC
    pad_heads = lambda a: jnp.pad(a.reshape(1, -1), ((0, 0), (0, LANES - SSM_HEADS)))
    mod = _ada_call(c, w_ada[l], b_ada[l])
    modrows = jnp.pad(jnp.transpose(mod, (1, 0, 2)), ((0, 0), (0, SUBLANES - 6), (0, 0)))
    w_main = w_in[l][:, :n_main].astype(BF16)
    w_dt = jnp.pad(w_in[l][:, n_main:], ((0, 0), (0, LANES - SSM_HEADS))).astype(BF16)
    wr_t = jnp.transpose(w_router[l])
    wr_hi = wr_t.astype(BF16)
    wr_split = jnp.stack([wr_hi, (wr_t - wr_hi.astype(F32)).astype(BF16)])
    x1, h2, logits_t = _mix_call(
        x, modrows, norm_mix_g[l].reshape(1, -1), norm_ffn_g[l].reshape(1, -1), w_main, w_dt,
        conv_a_w[l], ssm_conv_w[l], ssm_conv_b[l].reshape(1, -1), pad_heads(ssm_dt_bias[l]),
        pad_heads(-jnp.exp(ssm_a_log[l])), jnp.repeat(ssm_d[l], SSM_HEADDIM).reshape(1, -1),
        ssm_norm_g[l].reshape(1, -1), w_out[l][:D_CONV].astype(BF16),
        w_out[l][D_CONV:].astype(BF16), wr_split, _tile(seq, 512))
    h2 = h2.reshape(n_tok, D_MODEL)
    gates_t, cnt = _route_call(logits_t, router_bias[l].reshape(-1, 1), ts)
    meta, n_rows = _dispatch_plan(cnt[:, :, 0], ts, n_tok)
    xs = _dispatch_call(meta, h2, gates_t, ts, n_rows)
    ys = _expert_call(meta, xs, we_gate[l], we_up[l], we_down[l])
    sgu = jnp.concatenate([ws_gate[l], ws_up[l]], axis=-1).astype(BF16)
    out = _combine_call(meta, jnp.transpose(gates_t), h2, x1.reshape(n_tok, D_MODEL), modrows,
                        sgu, ws_down[l].astype(BF16), norm_final_g.reshape(1, -1), ys, ts, seq)
    return out.reshape(bsz, seq, D_MODEL)
```
